```python
import jax
import jax.numpy as jnp
from jax import lax
import numpy as np

D_MODEL = 1024
BATCH = 4
SEQ = 4096
DEPTH = 1
DEC_BATCH = 128
DEC_SEQ = 4
PAST_LEN = 8192
PAGE_SIZE = 128

MIX_WIDTH = D_MODEL
ATTN_WIDTH = MIX_WIDTH // 2
POOL_WIDTH = MIX_WIDTH - ATTN_WIDTH
HEAD_DIM = 64
N_HEADS = ATTN_WIDTH // HEAD_DIM
POOL_WINDOWS = (2, 4, 8, 16)
N_POOL_GROUPS = len(POOL_WINDOWS)
POOL_GROUP_DIM = POOL_WIDTH // N_POOL_GROUPS
POOL_STATE = max(POOL_WINDOWS) - 1
D_FF = ((8 * D_MODEL // 3 + 255) // 256) * 256
Q_BLOCK = 128
RMS_EPS = 1e-6
N_SUBLAYERS = 3
IN_COLS = 3 * ATTN_WIDTH + N_HEADS + POOL_WIDTH

kernel_name = 'fox_pool_parallel_macaron_decoder'


def _rmsnorm(x, g):
    xf = x.astype(jnp.float32)
    y = xf * lax.rsqrt(jnp.mean(xf * xf, axis=-1, keepdims=True) + RMS_EPS)
    return (y * g.astype(jnp.float32)).astype(x.dtype)


def _modnorm(x, g, shift, scale):
    return _rmsnorm(x, g) * (1 + scale) + shift


def _swiglu(h, w1, w3, w2):
    return (jax.nn.silu(h @ w1) * (h @ w3)) @ w2


def _ada_params(c, w_ada, b_ada):
    m = (jax.nn.silu(c) @ w_ada + b_ada)[:, None, :]
    return jnp.split(m, 3 * N_SUBLAYERS, axis=-1)


def _ffn_sublayer(x, g, mods, w1, w3, w2):
    shift, scale, gate = mods
    return x + 0.5 * gate * _swiglu(_modnorm(x, g, shift, scale), w1, w3, w2)


def _split_proj(h, w_in, b_f):
    z = h @ w_in
    lead = z.shape[:-1]
    a = ATTN_WIDTH
    q = z[..., :a].reshape(lead + (N_HEADS, HEAD_DIM))
    k = z[..., a:2 * a].reshape(lead + (N_HEADS, HEAD_DIM))
    v = z[..., 2 * a:3 * a].reshape(lead + (N_HEADS, HEAD_DIM))
    fl = z[..., 3 * a:3 * a + N_HEADS]
    u = z[..., 3 * a + N_HEADS:]
    logf = jax.nn.log_sigmoid(fl.astype(jnp.float32) + b_f.astype(jnp.float32))
    return q, k, v, logf, u


def _pool_mixer(u_ext, valid_ext, w_pool, pool_scale):
    b, r_t, _ = u_ext.shape
    t = r_t - POOL_STATE
    uf = u_ext.astype(jnp.float32)
    cs = jnp.concatenate([jnp.zeros((b, 1, POOL_WIDTH), jnp.float32), jnp.cumsum(uf, axis=1)], axis=1)
    cn = jnp.concatenate([jnp.zeros((1,), jnp.float32), jnp.cumsum(valid_ext)])
    u_new = uf[:, POOL_STATE:]
    hi = POOL_STATE + 1
    diffs = []
    for gi, w in enumerate(POOL_WINDOWS):
        ch = slice(gi * POOL_GROUP_DIM, (gi + 1) * POOL_GROUP_DIM)
        win_sum = cs[:, hi:hi + t, ch] - cs[:, hi - w:hi - w + t, ch]
        count = cn[hi:hi + t] - cn[hi - w:hi - w + t]
        diffs.append(win_sum / count[None, :, None] - u_new[:, :, ch])
    d = jnp.stack(diffs, axis=2)
    y = jnp.einsum('btgc,gcd->btgd', d, w_pool.astype(jnp.float32)).reshape(b, t, POOL_WIDTH)
    return (y * pool_scale.astype(jnp.float32)).astype(u_ext.dtype)


def _fox_prompt(q, k, v, logf):
    b, s_len, h, dh = q.shape
    nb = s_len // Q_BLOCK
    cum_f = jnp.cumsum(logf, axis=1)
    qf = (q.astype(jnp.float32) * (dh ** -0.5)).reshape(b, nb, Q_BLOCK, h, dh).swapaxes(0, 1)
    fq = cum_f.reshape(b, nb, Q_BLOCK, h).swapaxes(0, 1)
    kf = k.astype(jnp.float32)
    vf = v.astype(jnp.float32)
    fk = cum_f.transpose(0, 2, 1)[:, :, None, :]
    key_pos = jnp.arange(s_len)

    def block(args):
        qb, fb, bi = args
        sc = jnp.einsum('bthd,bshd->bhts', qb, kf) + fb.transpose(0, 2, 1)[..., None] - fk
        q_pos = bi * Q_BLOCK + jnp.arange(Q_BLOCK)
        sc = jnp.where(key_pos[None, :] <= q_pos[:, None], sc, -jnp.inf)
        p = jax.nn.softmax(sc, axis=-1)
        return jnp.einsum('bhts,bshd->bthd', p, vf)

    o = lax.map(block, (qf, fq, jnp.arange(nb)))
    return o.swapaxes(0, 1).reshape(b, s_len, h * dh).astype(q.dtype)


def _fox_sample(q, k_new, v_new, lf_new, cache_k, cache_v, cache_logf, page_table, layer):
    bd, t, h, dh = q.shape
    n_pages = page_table.shape[1]
    psz = cache_k.shape[2]
    qf = q.astype(jnp.float32) * (dh ** -0.5)
    c_new = jnp.cumsum(lf_new, axis=1)
    c_q = c_new.transpose(0, 2, 1)[..., None]
    lf_past = cache_logf[layer, page_table].astype(jnp.float32).reshape(bd, n_pages * psz, h)
    r_past = lax.cumsum(lf_past, axis=1, reverse=True) - lf_past
    r_pages = r_past.reshape(bd, n_pages, psz, h).swapaxes(0, 1)
    s_new = (jnp.einsum('bthd,bshd->bhts', qf, k_new.astype(jnp.float32))
             + c_q - c_new.transpose(0, 2, 1)[:, :, None, :])
    causal = jnp.arange(t)[None, :] <= jnp.arange(t)[:, None]
    s_new = jnp.where(causal, s_new, -jnp.inf)
    m0 = jnp.max(s_new, axis=-1)
    p0 = jnp.exp(s_new - m0[..., None])
    l0 = jnp.sum(p0, axis=-1)
    acc0 = jnp.einsum('bhts,bshd->bhtd', p0, v_new.astype(jnp.float32))

    def page_step(carry, xs):
        m, l, acc = carry
        pid, r = xs
        kp = cache_k[layer, pid].astype(jnp.float32)
        vp = cache_v[layer, pid].astype(jnp.float32)
        sp = jnp.einsum('bthd,bshd->bhts', qf, kp) + r.transpose(0, 2, 1)[:, :, None, :] + c_q
        m_new = jnp.maximum(m, jnp.max(sp, axis=-1))
        alpha = jnp.exp(m - m_new)
        pp = jnp.exp(sp - m_new[..., None])
        l_new = l * alpha + jnp.sum(pp, axis=-1)
        acc_new = acc * alpha[..., None] + jnp.einsum('bhts,bshd->bhtd', pp, vp)
        return (m_new, l_new, acc_new), None

    (_, l_f, acc_f), _ = lax.scan(page_step, (m0, l0, acc0), (page_table.T, r_pages))
    o = (acc_f / l_f[..., None]).transpose(0, 2, 1, 3).reshape(bd, t, h * dh)
    return o.astype(q.dtype)


def setup_inputs(seed: int = 0) -> dict:
    key = jax.random.key(seed)
    ks = jax.random.split(key, 32)
    f32 = jnp.float32
    n_pages = PAST_LEN // PAGE_SIZE
    n_used = DEC_BATCH * n_pages
    n_phys = n_used + max(1, n_used // 4)
    sd = D_MODEL ** -0.5

    def nrm(k, shape, scale):
        return jax.random.normal(k, shape, f32) * scale

    page_table = jax.random.permutation(ks[0], n_phys)[:n_used].reshape(DEC_BATCH, n_pages).astype(jnp.int32)
    return {
        'x_prompt': nrm(ks[1], (BATCH, SEQ, D_MODEL), 1.0),
        'x_sample': nrm(ks[2], (DEC_BATCH, DEC_SEQ, D_MODEL), 1.0),
        'cache_k': nrm(ks[3], (DEPTH, n_phys, PAGE_SIZE, N_HEADS, HEAD_DIM), 1.0),
        'cache_v': nrm(ks[4], (DEPTH, n_phys, PAGE_SIZE, N_HEADS, HEAD_DIM), 1.0),
        'cache_logf': jax.nn.log_sigmoid(9.0 + nrm(ks[5], (DEPTH, n_phys, PAGE_SIZE, N_HEADS), 0.5)),
        'state_pool': nrm(ks[6], (DEPTH, DEC_BATCH, POOL_STATE, POOL_WIDTH), 1.0),
        'page_table': page_table,
        'c_prompt': nrm(ks[7], (BATCH, D_MODEL), 1.0),
        'c_sample': nrm(ks[8], (DEC_BATCH, D_MODEL), 1.0),
        'norm_ffn1': 1.0 + nrm(ks[9], (DEPTH, D_MODEL), 0.05),
        'norm_mix': 1.0 + nrm(ks[10], (DEPTH, D_MODEL), 0.05),
        'norm_ffn2': 1.0 + nrm(ks[11], (DEPTH, D_MODEL), 0.05),
        'norm_final': 1.0 + nrm(ks[12], (D_MODEL,), 0.05),
        'w_ada': nrm(ks[13], (DEPTH, D_MODEL, 3 * N_SUBLAYERS * D_MODEL), 0.5 * sd),
        'b_ada': nrm(ks[14], (DEPTH, 3 * N_SUBLAYERS * D_MODEL), 0.02),
        'ffn1_w1': nrm(ks[15], (DEPTH, D_MODEL, D_FF), sd),
        'ffn1_w3': nrm(ks[16], (DEPTH, D_MODEL, D_FF), sd),
        'ffn1_w2': nrm(ks[17], (DEPTH, D_FF, D_MODEL), D_FF ** -0.5),
        'ffn2_w1': nrm(ks[18], (DEPTH, D_MODEL, D_FF), sd),
        'ffn2_w3': nrm(ks[19], (DEPTH, D_MODEL, D_FF), sd),
        'ffn2_w2': nrm(ks[20], (DEPTH, D_FF, D_MODEL), D_FF ** -0.5),
        'w_in': nrm(ks[21], (DEPTH, D_MODEL, IN_COLS), sd),
        'b_f': 1.0 + 4.0 * jax.random.uniform(ks[22], (DEPTH, N_HEADS), f32),
        'w_o': nrm(ks[23], (DEPTH, MIX_WIDTH, D_MODEL), MIX_WIDTH ** -0.5),
        'w_pool': nrm(ks[24], (DEPTH, N_POOL_GROUPS, POOL_GROUP_DIM, POOL_GROUP_DIM), POOL_GROUP_DIM ** -0.5),
        'pool_scale': 1.0 + nrm(ks[25], (DEPTH, POOL_WIDTH), 0.1),
    }


def reference(x_prompt, x_sample, cache_k, cache_v, cache_logf, state_pool, page_table,
              c_prompt, c_sample, norm_ffn1, norm_mix, norm_ffn2, norm_final, w_ada, b_ada,
              ffn1_w1, ffn1_w3, ffn1_w2, ffn2_w1, ffn2_w3, ffn2_w2, w_in, b_f, w_o,
              w_pool, pool_scale):
    bp, s_len, _ = x_prompt.shape
    bd, t_len, _ = x_sample.shape
    xp = x_prompt
    xs = x_sample
    kp_l, vp_l, fp_l, pp_l = [], [], [], []
    ks_l, vs_l, fs_l, ps_l = [], [], [], []
    valid_p = jnp.concatenate([jnp.zeros((POOL_STATE,), jnp.float32), jnp.ones((s_len,), jnp.float32)])
    valid_s = jnp.ones((POOL_STATE + t_len,), jnp.float32)
    for l in range(DEPTH):
        mods_p = _ada_params(c_prompt, w_ada[l], b_ada[l])
        mods_s = _ada_params(c_sample, w_ada[l], b_ada[l])

        xp = _ffn_sublayer(xp, norm_ffn1[l], mods_p[0:3], ffn1_w1[l], ffn1_w3[l], ffn1_w2[l])
        hp = _modnorm(xp, norm_mix[l], mods_p[3], mods_p[4])
        q, k, v, logf, u = _split_proj(hp, w_in[l], b_f[l])
        attn = _fox_prompt(q, k, v, logf)
        u_ext = jnp.concatenate([jnp.zeros((bp, POOL_STATE, POOL_WIDTH), u.dtype), u], axis=1)
        pool = _pool_mixer(u_ext, valid_p, w_pool[l], pool_scale[l])
        xp = xp + mods_p[5] * (jnp.concatenate([attn, pool], axis=-1) @ w_o[l])
        xp = _ffn_sublayer(xp, norm_ffn2[l], mods_p[6:9], ffn2_w1[l], ffn2_w3[l], ffn2_w2[l])
        n_pp = s_len // PAGE_SIZE
        kp_l.append(k.reshape(bp, n_pp, PAGE_SIZE, N_HEADS, HEAD_DIM))
        vp_l.append(v.reshape(bp, n_pp, PAGE_SIZE, N_HEADS, HEAD_DIM))
        fp_l.append(logf.reshape(bp, n_pp, PAGE_SIZE, N_HEADS))
        pp_l.append(u[:, s_len - POOL_STATE:])

        xs = _ffn_sublayer(xs, norm_ffn1[l], mods_s[0:3], ffn1_w1[l], ffn1_w3[l], ffn1_w2[l])
        hs = _modnorm(xs, norm_mix[l], mods_s[3], mods_s[4])
        q, k, v, logf, u = _split_proj(hs, w_in[l], b_f[l])
        attn = _fox_sample(q, k, v, logf, cache_k, cache_v, cache_logf, page_table, l)
        u_ext = jnp.concatenate([state_pool[l].astype(u.dtype), u], axis=1)
        pool = _pool_mixer(u_ext, valid_s, w_pool[l], pool_scale[l])
        xs = xs + mods_s[5] * (jnp.concatenate([attn, pool], axis=-1) @ w_o[l])
        xs = _ffn_sublayer(xs, norm_ffn2[l], mods_s[6:9], ffn2_w1[l], ffn2_w3[l], ffn2_w2[l])
        ks_l.append(k)
        vs_l.append(v)
        fs_l.append(logf)
        ps_l.append(u_ext[:, t_len:])

    y_prompt = _rmsnorm(xp, norm_final)
    y_sample = _rmsnorm(xs, norm_final)
    new_k_prompt = jnp.stack(kp_l)
    new_v_prompt = jnp.stack(vp_l)
    new_logf_prompt = jnp.stack(fp_l)
    new_pool_prompt = jnp.stack(pp_l)
    new_k_sample = jnp.stack(ks_l)
    new_v_sample = jnp.stack(vs_l)
    new_logf_sample = jnp.stack(fs_l)
    new_pool_sample = jnp.stack(ps_l)
    return (y_prompt, y_sample, new_k_prompt, new_v_prompt, new_logf_prompt, new_pool_prompt,
            new_k_sample, new_v_sample, new_logf_sample, new_pool_sample)
```

```python
import functools

import jax
import jax.numpy as jnp
from jax import lax
from jax.experimental import pallas as pl
from jax.experimental.pallas import tpu as pltpu

F32 = jnp.float32
BF16 = jnp.bfloat16

RMS_EPS = 1e-6
POOL_WINDOWS = (2, 4, 8, 16)
N_MODS = 9
LANES = 128
VMEM_LIMIT_BYTES = 56 * 1024 * 1024

ROW_TILE = 512
FFN_CHUNK = 256
ATTN_TILE = 512
DECODE_PAGES_PER_STEP = 8

_NT_DIMS = (((1,), (1,)), ((), ()))


def _dot(a, b):
    return jnp.dot(a, b, preferred_element_type=F32)


def _dot_nt(a, b):
    return lax.dot_general(a, b, _NT_DIMS, preferred_element_type=F32)


def _split3(x):
    hi = x.astype(BF16)
    r = x - hi.astype(F32)
    mid = r.astype(BF16)
    lo = (r - mid.astype(F32)).astype(BF16)
    return hi, mid, lo


def _dot_exact_left(x, ones_mat):
    hi, mid, lo = _split3(x)
    return _dot(hi, ones_mat) + _dot(mid, ones_mat) + _dot(lo, ones_mat)


def _dot_exact_right(ones_mat, x):
    hi, mid, lo = _split3(x)
    return _dot(ones_mat, hi) + _dot(ones_mat, mid) + _dot(ones_mat, lo)


def _silu(x):
    return x * jax.nn.sigmoid(x)


def _log_sigmoid(x):
    return jnp.minimum(x, 0.0) - jnp.log1p(jnp.exp(-jnp.abs(x)))


def _rmsnorm(x, g):
    return x * lax.rsqrt(jnp.mean(x * x, axis=-1, keepdims=True) + RMS_EPS) * g


def _rows(m, n):
    if m.shape[0] in (1, n):
        return m
    return jnp.concatenate([m] * (n // m.shape[0]), axis=0)


def _modnorm(x, g, shift, scale):
    n = x.shape[0]
    return _rmsnorm(x, g) * (1.0 + _rows(scale, n)) + _rows(shift, n)


def _resident(shape):
    zeros = (0,) * len(shape)
    return pl.BlockSpec(shape, lambda *_: zeros, pipeline_mode=pl.Buffered(1))


def _params():
    return pltpu.CompilerParams(vmem_limit_bytes=VMEM_LIMIT_BYTES)


def _ada_kernel(c_ref, w_ref, b_ref, o_ref):
    s = _silu(c_ref[...]).astype(BF16)
    o_ref[...] = _dot(s, w_ref[...].astype(BF16)) + b_ref[...]


def _ada(c, w, b):
    n, d = c.shape
    m = w.shape[1]
    tn = d
    return pl.pallas_call(
        _ada_kernel,
        grid=(m // tn,),
        in_specs=[
            pl.BlockSpec((n, d), lambda j: (0, 0)),
            pl.BlockSpec((d, tn), lambda j: (0, j)),
            pl.BlockSpec((1, tn), lambda j: (0, j)),
        ],
        out_specs=pl.BlockSpec((n, tn), lambda j: (0, j)),
        out_shape=jax.ShapeDtypeStruct((n, m), F32),
        compiler_params=_params(),
        name="ada_params",
    )(c, w, b)


def _ffn_kernel(x_ref, g_ref, sh_ref, sc_ref, gt_ref, w1_ref, w3_ref, w2_ref, *rest, final_norm):
    if final_norm:
        gf_ref, o_ref, h_ref, acc_ref = rest
    else:
        o_ref, h_ref, acc_ref = rest
    x = x_ref[...]
    n = x.shape[0]
    h_ref[...] = _modnorm(x, g_ref[...], sh_ref[...], sc_ref[...]).astype(BF16)
    acc_ref[...] = jnp.zeros_like(acc_ref)

    def chunk(c, carry):
        off = pl.multiple_of(c * FFN_CHUNK, FFN_CHUNK)
        h = h_ref[...]
        a = _dot(h, w1_ref[:, pl.ds(off, FFN_CHUNK)])
        b = _dot(h, w3_ref[:, pl.ds(off, FFN_CHUNK)])
        act = (_silu(a) * b).astype(BF16)
        acc_ref[...] += _dot(act, w2_ref[pl.ds(off, FFN_CHUNK), :])
        return carry

    lax.fori_loop(0, w1_ref.shape[1] // FFN_CHUNK, chunk, 0)
    out = x + 0.5 * _rows(gt_ref[...], n) * acc_ref[...]
    if final_norm:
        out = _rmsnorm(out, gf_ref[...])
    o_ref[...] = out


def _ffn(x, g, mods, mod_spec, k0, w1, w3, w2, gf=None):
    r, d = x.shape
    dff = w1.shape[1]
    tm = ROW_TILE
    in_specs = [
        pl.BlockSpec((tm, d), lambda i: (i, 0)),
        _resident((1, d)),
        mod_spec(k0),
        mod_spec(k0 + 1),
        mod_spec(k0 + 2),
        _resident((d, dff)),
        _resident((d, dff)),
        _resident((dff, d)),
    ]
    args = [x, g, mods, mods, mods, w1, w3, w2]
    if gf is not None:
        in_specs.append(_resident((1, d)))
        args.append(gf)
    return pl.pallas_call(
        functools.partial(_ffn_kernel, final_norm=gf is not None),
        grid=(r // tm,),
        in_specs=in_specs,
        out_specs=pl.BlockSpec((tm, d), lambda i: (i, 0)),
        out_shape=jax.ShapeDtypeStruct((r, d), F32),
        scratch_shapes=[pltpu.VMEM((tm, d), BF16), pltpu.VMEM((tm, d), F32)],
        compiler_params=_params(),
        name="ffn_final" if gf is not None else "ffn",
    )(*args)


def _inproj_prompt_kernel(x_ref, g_ref, sh_ref, sc_ref, wq_ref, wkT_ref, wvT_ref, wf_ref, wfT_ref,
                          bf_ref, bfT_ref, wu_ref,
                          q_ref, kT_ref, vT_ref, kTb_ref, vTb_ref, lfT_ref, cf_ref, cfT_ref, u_ref,
                          carry_ref, carryT_ref, *, q_scale):
    @pl.when(pl.program_id(1) == 0)
    def _():
        carry_ref[...] = jnp.zeros_like(carry_ref)
        carryT_ref[...] = jnp.zeros_like(carryT_ref)

    tm = x_ref.shape[0]
    page = kT_ref.shape[-1]
    n_heads = cf_ref.shape[-1]
    h = _modnorm(x_ref[...], g_ref[...], sh_ref[...], sc_ref[...]).astype(BF16)
    q_ref[...] = (_dot(h, wq_ref[...]) * q_scale).astype(BF16)
    u_ref[...] = _dot(h, wu_ref[...])
    kT = _dot_nt(wkT_ref[...], h)
    vT = _dot_nt(wvT_ref[...], h)
    kTb_ref[...] = kT.astype(BF16)
    vTb_ref[...] = vT.astype(BF16)
    lf = _log_sigmoid(_dot(h, wf_ref[...]) + bf_ref[...])
    lfT = _log_sigmoid(_dot_nt(wfT_ref[...], h) + bfT_ref[...])
    for j in range(tm // page):
        cols = slice(j * page, (j + 1) * page)
        kT_ref[j] = kT[:, cols]
        vT_ref[j] = vT[:, cols]
        lfT_ref[j] = lfT[:, cols]
    r = lax.broadcasted_iota(jnp.int32, (tm, tm), 0)
    c = lax.broadcasted_iota(jnp.int32, (tm, tm), 1)
    cf = _dot_exact_right((c <= r).astype(BF16), lf) + carry_ref[...]
    cf_ref[...] = cf[:, :n_heads]
    carry_ref[...] = cf[tm - 1:tm, :]
    cfT = _dot_exact_left(lfT, (r <= c).astype(BF16)) + carryT_ref[:, :1]
    cfT_ref[...] = cfT
    carryT_ref[...] = jnp.broadcast_to(cfT[:, tm - 1:tm], carryT_ref.shape)


def _inproj_prompt(x, g, mods, k0, wq, wkT, wvT, wf, wfT, bf, bfT, wu, page, n_heads, q_scale):
    b, s, d = x.shape
    a = wq.shape[1]
    pw = wu.shape[1]
    tm = ROW_TILE
    npg = tm // page

    def mod_spec(k):
        return pl.BlockSpec((None, 1, d), lambda bi, ti: (bi, 0, k))

    out_shape = (
        jax.ShapeDtypeStruct((b, s, a), BF16),
        jax.ShapeDtypeStruct((b, s // page, a, page), F32),
        jax.ShapeDtypeStruct((b, s // page, a, page), F32),
        jax.ShapeDtypeStruct((b, a, s), BF16),
        jax.ShapeDtypeStruct((b, a, s), BF16),
        jax.ShapeDtypeStruct((b, s // page, n_heads, page), F32),
        jax.ShapeDtypeStruct((b, s, n_heads), F32),
        jax.ShapeDtypeStruct((b, n_heads, s), F32),
        jax.ShapeDtypeStruct((b, s, pw), F32),
    )
    out_specs = (
        pl.BlockSpec((None, tm, a), lambda bi, ti: (bi, ti, 0)),
        pl.BlockSpec((None, npg, a, page), lambda bi, ti: (bi, ti, 0, 0)),
        pl.BlockSpec((None, npg, a, page), lambda bi, ti: (bi, ti, 0, 0)),
        pl.BlockSpec((None, a, tm), lambda bi, ti: (bi, 0, ti)),
        pl.BlockSpec((None, a, tm), lambda bi, ti: (bi, 0, ti)),
        pl.BlockSpec((None, npg, n_heads, page), lambda bi, ti: (bi, ti, 0, 0)),
        pl.BlockSpec((None, tm, n_heads), lambda bi, ti: (bi, ti, 0)),
        pl.BlockSpec((None, n_heads, tm), lambda bi, ti: (bi, 0, ti)),
        pl.BlockSpec((None, tm, pw), lambda bi, ti: (bi, ti, 0)),
    )
    in_specs = [
        pl.BlockSpec((None, tm, d), lambda bi, ti: (bi, ti, 0)),
        _resident((1, d)),
        mod_spec(k0),
        mod_spec(k0 + 1),
        _resident(wq.shape),
        _resident(wkT.shape),
        _resident(wvT.shape),
        _resident(wf.shape),
        _resident(wfT.shape),
        _resident(bf.shape),
        _resident(bfT.shape),
        _resident(wu.shape),
    ]
    return pl.pallas_call(
        functools.partial(_inproj_prompt_kernel, q_scale=q_scale),
        grid=(b, s // tm),
        in_specs=in_specs,
        out_specs=out_specs,
        out_shape=out_shape,
        scratch_shapes=[pltpu.VMEM((1, LANES), F32), pltpu.VMEM((n_heads, LANES), F32)],
        compiler_params=_params(),
        name="inproj_prompt",
    )(x, g, mods, mods, wq, wkT, wvT, wf, wfT, bf, bfT, wu)


def _attn_prompt_kernel(q_ref, kT_ref, vT_ref, cf_ref, cfT_ref, o_ref, m_ref, l_ref, acc_ref, *, head_dim):
    hp = pl.program_id(1)
    qi = pl.program_id(2)
    tq = q_ref.shape[0]
    tk = tq
    q = q_ref[...]
    lane = lax.broadcasted_iota(jnp.int32, (1, q.shape[1]), 1)
    first = lane < head_dim
    zero = jnp.zeros_like(q)
    qs = (jnp.where(first, q, zero), jnp.where(first, zero, q))
    cf = cf_ref[...]
    head_lane = lax.broadcasted_iota(jnp.int32, (1, cf.shape[1]), 1)
    fq = [jnp.sum(jnp.where(head_lane == 2 * hp + i, cf, 0.0), axis=1, keepdims=True) for i in range(2)]
    m_ref[...] = jnp.full_like(m_ref, -jnp.inf)
    l_ref[...] = jnp.zeros_like(l_ref)
    acc_ref[...] = jnp.zeros_like(acc_ref)

    def step(j, masked):
        off = pl.multiple_of(j * tk, tk)
        kT = kT_ref[:, pl.ds(off, tk)]
        vT = vT_ref[:, pl.ds(off, tk)]
        for i in range(2):
            fk = cfT_ref[pl.ds(2 * hp + i, 1), pl.ds(off, tk)]
            s = _dot(qs[i], kT) + (fq[i] - fk)
            if masked:
                row = lax.broadcasted_iota(jnp.int32, (tq, tk), 0)
                col = lax.broadcasted_iota(jnp.int32, (tq, tk), 1)
                s = jnp.where(col <= row, s, -jnp.inf)
            m_prev = m_ref[i]
            m_new = jnp.maximum(m_prev, jnp.max(s, axis=1, keepdims=True))
            alpha = jnp.exp(m_prev - m_new)
            p = jnp.exp(s - m_new)
            l_ref[i] = alpha * l_ref[i] + jnp.sum(p, axis=1, keepdims=True)
            acc_ref[i] = alpha * acc_ref[i] + _dot_nt(p.astype(BF16), vT)
            m_ref[i] = m_new

    def full_block(j, carry):
        step(j, False)
        return carry

    lax.fori_loop(0, qi, full_block, 0)
    step(qi, True)
    o = [acc_ref[i] / l_ref[i] for i in range(2)]
    o_ref[...] = jnp.where(first, o[0], o[1]).astype(BF16)


def _attn_prompt(q, kTb, vTb, cf, cfT, head_dim):
    b, s, a = q.shape
    n_heads = cf.shape[-1]
    tq = ATTN_TILE
    blk = 2 * head_dim
    return pl.pallas_call(
        functools.partial(_attn_prompt_kernel, head_dim=head_dim),
        grid=(b, a // blk, s // tq),
        in_specs=[
            pl.BlockSpec((None, tq, blk), lambda bi, hp, qi: (bi, qi, hp)),
            pl.BlockSpec((None, blk, s), lambda bi, hp, qi: (bi, hp, 0)),
            pl.BlockSpec((None, blk, s), lambda bi, hp, qi: (bi, hp, 0)),
            pl.BlockSpec((None, tq, n_heads), lambda bi, hp, qi: (bi, qi, 0)),
            pl.BlockSpec((None, n_heads, s), lambda bi, hp, qi: (bi, 0, 0)),
        ],
        out_specs=pl.BlockSpec((None, tq, blk), lambda bi, hp, qi: (bi, qi, hp)),
        out_shape=jax.ShapeDtypeStruct((b, s, a), BF16),
        scratch_shapes=[
            pltpu.VMEM((2, tq, 1), F32),
            pltpu.VMEM((2, tq, 1), F32),
            pltpu.VMEM((2, tq, blk), F32),
        ],
        compiler_params=_params(),
        name="attn_prompt",
    )(q, kTb, vTb, cf, cfT)


def _pool_prompt_kernel(u_ref, wp_ref, ps_ref, o_ref, halo_ref):
    ti = pl.program_id(1)

    @pl.when(ti == 0)
    def _():
        halo_ref[...] = jnp.zeros_like(halo_ref)

    tm = u_ref.shape[0]
    nh = halo_ref.shape[0]
    gd = wp_ref.shape[1]
    u = u_ref[...]
    ext = jnp.concatenate([halo_ref[...], u], axis=0)
    halo_ref[...] = u[tm - nh:, :]
    sums = {1: ext}
    w = 1
    while w < max(POOL_WINDOWS):
        sums[2 * w] = sums[w] + pltpu.roll(sums[w], w, 0)
        w *= 2
    pos = ti * tm + lax.broadcasted_iota(jnp.int32, (tm, 1), 0)
    ys = []
    for gi, w in enumerate(POOL_WINDOWS):
        lanes = slice(gi * gd, (gi + 1) * gd)
        count = jnp.minimum(pos + 1, w).astype(F32)
        d = sums[w][nh:, lanes] / count - u[:, lanes]
        ys.append(_dot(d.astype(BF16), wp_ref[gi]))
    o_ref[...] = (jnp.concatenate(ys, axis=1) * ps_ref[...]).astype(BF16)


def _pool_prompt(u, wp, ps):
    b, s, pw = u.shape
    tm = ROW_TILE
    halo = 16
    assert halo >= max(POOL_WINDOWS)
    return pl.pallas_call(
        _pool_prompt_kernel,
        grid=(b, s // tm),
        in_specs=[
            pl.BlockSpec((None, tm, pw), lambda bi, ti: (bi, ti, 0)),
            _resident(wp.shape),
            _resident(ps.shape),
        ],
        out_specs=pl.BlockSpec((None, tm, pw), lambda bi, ti: (bi, ti, 0)),
        out_shape=jax.ShapeDtypeStruct((b, s, pw), BF16),
        scratch_shapes=[pltpu.VMEM((halo, pw), F32)],
        compiler_params=_params(),
        name="pool_prompt",
    )(u, wp, ps)


def _outproj_kernel(x_ref, a_ref, p_ref, gt_ref, wa_ref, wp_ref, o_ref):
    x = x_ref[...]
    mix = _dot(a_ref[...], wa_ref[...]) + _dot(p_ref[...], wp_ref[...])
    o_ref[...] = x + _rows(gt_ref[...], x.shape[0]) * mix


def _outproj(x, attn, pool, mods, mod_spec, k, wa, wp):
    r, d = x.shape
    tm = ROW_TILE
    return pl.pallas_call(
        _outproj_kernel,
        grid=(r // tm,),
        in_specs=[
            pl.BlockSpec((tm, d), lambda i: (i, 0)),
            pl.BlockSpec((tm, attn.shape[1]), lambda i: (i, 0)),
            pl.BlockSpec((tm, pool.shape[1]), lambda i: (i, 0)),
            mod_spec(k),
            _resident(wa.shape),
            _resident(wp.shape),
        ],
        out_specs=pl.BlockSpec((tm, d), lambda i: (i, 0)),
        out_shape=jax.ShapeDtypeStruct((r, d), F32),
        compiler_params=_params(),
        name="outproj",
    )(x, attn, pool, mods, wa, wp)


def _inproj_sample_kernel(x_ref, g_ref, sh_ref, sc_ref, wqkv_ref, wkT_ref, wvT_ref, wf_ref, wfT_ref,
                          bf_ref, bfT_ref, wu_ref,
                          q_ref, k_ref, v_ref, kT_ref, vT_ref, lfT_ref, c_ref, u_ref, *, q_scale):
    n_t, _, nb = kT_ref.shape
    n_heads = c_ref.shape[-1]
    h = _modnorm(x_ref[...], g_ref[...], sh_ref[...], sc_ref[...]).astype(BF16)
    z = _dot(h, wqkv_ref[...])
    a = z.shape[1] // 3
    q_ref[...] = (z[:, :a] * q_scale).astype(BF16)
    k_ref[...] = z[:, a:2 * a]
    v_ref[...] = z[:, 2 * a:]
    u_ref[...] = _dot(h, wu_ref[...])
    kT = _dot_nt(wkT_ref[...], h)
    vT = _dot_nt(wvT_ref[...], h)
    lf = _log_sigmoid(_dot(h, wf_ref[...]) + bf_ref[...])
    lfT = _log_sigmoid(_dot_nt(wfT_ref[...], h) + bfT_ref[...])
    run = None
    parts = []
    for t in range(n_t):
        cols = slice(t * nb, (t + 1) * nb)
        kT_ref[t] = kT[:, cols]
        vT_ref[t] = vT[:, cols]
        lfT_ref[t] = lfT[:, cols]
        blk = lf[t * nb:(t + 1) * nb, :]
        run = blk if run is None else run + blk
        parts.append(run)
    c_ref[...] = jnp.concatenate(parts, axis=0)[:, :n_heads]


def _inproj_sample(x, g, mods, mod_spec, k0, wqkv, wkT, wvT, wf, wfT, bf, bfT, wu, n_t, n_heads, q_scale):
    r, d = x.shape
    a = wkT.shape[0]
    pw = wu.shape[1]
    nb = r // n_t
    out_shape = (
        jax.ShapeDtypeStruct((r, a), BF16),
        jax.ShapeDtypeStruct((r, a), F32),
        jax.ShapeDtypeStruct((r, a), F32),
        jax.ShapeDtypeStruct((n_t, a, nb), F32),
        jax.ShapeDtypeStruct((n_t, a, nb), F32),
        jax.ShapeDtypeStruct((n_t, n_heads, nb), F32),
        jax.ShapeDtypeStruct((r, n_heads), F32),
        jax.ShapeDtypeStruct((r, pw), F32),
    )
    whole = lambda shp: pl.BlockSpec(shp, lambda i: (0,) * len(shp))
    in_specs = [whole((r, d)), whole((1, d)), mod_spec(k0), mod_spec(k0 + 1)] + [
        whole(w.shape) for w in (wqkv, wkT, wvT, wf, wfT, bf, bfT, wu)
    ]
    return pl.pallas_call(
        functools.partial(_inproj_sample_kernel, q_scale=q_scale),
        grid=(1,),
        in_specs=in_specs,
        out_specs=tuple(whole(o.shape) for o in out_shape),
        out_shape=out_shape,
        compiler_params=_params(),
        name="inproj_sample",
    )(x, g, mods, mods, wqkv, wkT, wvT, wf, wfT, bf, bfT, wu)


def _decode_kernel(pt_ref, q_ref, cq_ref, c2_ref, kn_ref, vn_ref, *rest, pages_per_step, head_dim):
    del pt_ref
    g_n = pages_per_step
    k_refs = rest[:g_n]
    v_refs = rest[g_n:2 * g_n]
    lf_refs = rest[2 * g_n:3 * g_n]
    o_ref, m_ref, l_ref, acc_ref, r_ref = rest[3 * g_n:]
    step = pl.program_id(1)
    n_rows, width = q_ref.shape
    n_heads = width // head_dim
    n_t = n_rows // n_heads
    row = lax.broadcasted_iota(jnp.int32, (n_rows, 1), 0)
    lane = lax.broadcasted_iota(jnp.int32, (1, width), 1)
    own = (lane // head_dim) == (row % n_heads)
    t_row = row // n_heads
    q = q_ref[...]
    qbd = jnp.where(own, q, jnp.zeros_like(q))
    cq = cq_ref[...]

    @pl.when(step == 0)
    def _():
        qf = qbd.astype(F32)
        kn = kn_ref[...]
        vn = vn_ref[...]
        c2 = c2_ref[...]
        sc = []
        for s in range(n_t):
            col = jnp.sum(qf * kn[s:s + 1, :], axis=1, keepdims=True) + cq - c2[:, s:s + 1]
            sc.append(jnp.where(t_row >= s, col, -jnp.inf))
        m0 = functools.reduce(jnp.maximum, sc)
        l0 = jnp.zeros_like(m0)
        a0 = jnp.zeros(acc_ref.shape, F32)
        for s in range(n_t):
            p = jnp.exp(sc[s] - m0)
            l0 = l0 + p
            a0 = a0 + p * vn[s:s + 1, :]
        m_ref[...] = m0
        l_ref[...] = l0
        acc_ref[...] = a0
        r_ref[...] = jnp.zeros_like(r_ref)

    page = lf_refs[0].shape[-1]
    pj = lax.broadcasted_iota(jnp.int32, (page, page), 0)
    ps = lax.broadcasted_iota(jnp.int32, (page, page), 1)
    later = (pj > ps).astype(BF16)
    r_run = r_ref[...]
    scores = []
    for i in range(g_n):
        lf = lf_refs[i][...]
        suffix = _dot_exact_left(lf, later) + r_run
        r_run = r_run + jnp.sum(lf, axis=1, keepdims=True)
        bias = jnp.concatenate([suffix] * n_t, axis=0) + cq
        scores.append(_dot(qbd, k_refs[i][...].astype(BF16)) + bias)
    r_ref[...] = r_run
    m_prev = m_ref[...]
    m_new = functools.reduce(jnp.maximum, [m_prev] + [jnp.max(s, axis=1, keepdims=True) for s in scores])
    alpha = jnp.exp(m_prev - m_new)
    l_new = alpha * l_ref[...]
    acc = alpha * acc_ref[...]
    for i in range(g_n):
        p = jnp.exp(scores[i] - m_new)
        l_new = l_new + jnp.sum(p, axis=1, keepdims=True)
        acc = acc + _dot_nt(p.astype(BF16), v_refs[i][...].astype(BF16))
    m_ref[...] = m_new
    l_ref[...] = l_new
    acc_ref[...] = acc

    @pl.when(step == pl.num_programs(1) - 1)
    def _():
        o = jnp.where(own, acc, 0.0) / l_new
        o_ref[...] = o.reshape(n_t, n_heads, width).sum(axis=1)


def _decode(page_table, q_rep, cq, c2, k_new, v_new, cache_kT, cache_vT, cache_lfT, head_dim):
    bd, n_pages = page_table.shape
    _, n_rows, width = q_rep.shape
    n_t = k_new.shape[1]
    page = cache_kT.shape[-1]
    n_heads = cache_lfT.shape[1]
    g_n = DECODE_PAGES_PER_STEP

    def page_spec(shape, i):
        def index(b, g, pt):
            return (pt[b, n_pages - 1 - (g * g_n + i)], 0, 0)
        return pl.BlockSpec((None,) + shape, index)

    per_seq = lambda shp: pl.BlockSpec((None,) + shp, lambda b, g, pt: (b, 0, 0))
    in_specs = [
        per_seq((n_rows, width)),
        per_seq((n_rows, 1)),
        per_seq((n_rows, n_t)),
        per_seq((n_t, width)),
        per_seq((n_t, width)),
    ]
    in_specs += [page_spec((width, page), i) for i in range(g_n)]
    in_specs += [page_spec((width, page), i) for i in range(g_n)]
    in_specs += [page_spec((n_heads, page), i) for i in range(g_n)]
    grid_spec = pltpu.PrefetchScalarGridSpec(
        num_scalar_prefetch=1,
        grid=(bd, n_pages // g_n),
        in_specs=in_specs,
        out_specs=per_seq((n_t, width)),
        scratch_shapes=[
            pltpu.VMEM((n_rows, 1), F32),
            pltpu.VMEM((n_rows, 1), F32),
            pltpu.VMEM((n_rows, width), F32),
            pltpu.VMEM((n_heads, LANES), F32),
        ],
    )
    return pl.pallas_call(
        functools.partial(_decode_kernel, pages_per_step=g_n, head_dim=head_dim),
        grid_spec=grid_spec,
        out_shape=jax.ShapeDtypeStruct((bd, n_t, width), F32),
        compiler_params=_params(),
        name="decode_attn",
    )(page_table, q_rep, cq, c2, k_new, v_new, *([cache_kT] * g_n), *([cache_vT] * g_n), *([cache_lfT] * g_n))


def _pool_sample_kernel(st_ref, u_ref, wp_ref, ps_ref, o_ref):
    n_state = st_ref.shape[0]
    n_t = u_ref.shape[0]
    gd = wp_ref.shape[1]

    def rows(idx, lanes):
        return st_ref[idx, :, lanes] if idx < n_state else u_ref[idx - n_state, :, lanes]

    for t in range(n_t):
        ys = []
        for gi, w in enumerate(POOL_WINDOWS):
            lanes = slice(gi * gd, (gi + 1) * gd)
            cur = rows(n_state + t, lanes)
            total = cur
            for j in range(1, w):
                total = total + rows(n_state + t - j, lanes)
            d = total / float(w) - cur
            ys.append(_dot(d.astype(BF16), wp_ref[gi]))
        o_ref[t] = (jnp.concatenate(ys, axis=1) * ps_ref[...]).astype(BF16)


def _pool_sample(state_tm, u_tm, wp, ps):
    assert state_tm.shape[0] >= max(POOL_WINDOWS) - 1
    whole = lambda shp: pl.BlockSpec(shp, lambda i: (0,) * len(shp))
    return pl.pallas_call(
        _pool_sample_kernel,
        grid=(1,),
        in_specs=[whole(state_tm.shape), whole(u_tm.shape), whole(wp.shape), whole(ps.shape)],
        out_specs=whole(u_tm.shape),
        out_shape=jax.ShapeDtypeStruct(u_tm.shape, BF16),
        compiler_params=_params(),
        name="pool_sample",
    )(state_tm, u_tm, wp, ps)


def kernel(x_prompt, x_sample, cache_k, cache_v, cache_logf, state_pool, page_table, c_prompt, c_sample, norm_ffn1, norm_mix, norm_ffn2, norm_final, w_ada, b_ada, ffn1_w1, ffn1_w3, ffn1_w2, ffn2_w1, ffn2_w3, ffn2_w2, w_in, b_f, w_o, w_pool, pool_scale):
    bp, s_len, d = x_prompt.shape
    bd, t_len, _ = x_sample.shape
    depth, n_phys, page, n_heads, head_dim = cache_k.shape
    assert depth == 1, "single-layer model"
    a = n_heads * head_dim
    pw = state_pool.shape[-1]
    n_state = state_pool.shape[2]
    q_scale = head_dim ** -0.5
    li = 0

    n_pad = (-(bd + bp)) % 16
    c_all = jnp.concatenate([c_sample, c_prompt, jnp.zeros((n_pad, d), F32)], axis=0)
    mods = _ada(c_all, w_ada[li], b_ada[li][None, :])
    mods_p = mods[bd:bd + bp].reshape(bp, 1, N_MODS * d)
    tiles_per_seq = s_len // ROW_TILE

    def prompt_mod(k):
        return pl.BlockSpec((None, 1, d), lambda i: (i // tiles_per_seq, 0, k))

    def sample_mod(k):
        return pl.BlockSpec((bd, d), lambda i: (0, k))

    f1 = [w[li].astype(BF16) for w in (ffn1_w1, ffn1_w3, ffn1_w2)]
    f2 = [w[li].astype(BF16) for w in (ffn2_w1, ffn2_w3, ffn2_w2)]
    w_in_l = w_in[li]
    wqkv = w_in_l[:, :3 * a].astype(BF16)
    wq = wqkv[:, :a]
    wkT = w_in_l[:, a:2 * a].T.astype(BF16)
    wvT = w_in_l[:, 2 * a:3 * a].T.astype(BF16)
    wf_cols = w_in_l[:, 3 * a:3 * a + n_heads]
    wf = jnp.pad(wf_cols, ((0, 0), (0, LANES - n_heads))).astype(BF16)
    wfT = wf_cols.T.astype(BF16)
    bf = jnp.pad(b_f[li][None, :], ((0, 0), (0, LANES - n_heads)))
    bfT = b_f[li][:, None]
    wu = w_in_l[:, 3 * a + n_heads:].astype(BF16)
    wo_a = w_o[li][:a].astype(BF16)
    wo_p = w_o[li][a:].astype(BF16)
    wp = w_pool[li].astype(BF16)
    ps = pool_scale[li][None, :]
    g1 = norm_ffn1[li][None, :]
    gm = norm_mix[li][None, :]
    g2 = norm_ffn2[li][None, :]
    gf = norm_final[None, :]

    xp = x_prompt.reshape(bp * s_len, d)
    xp = _ffn(xp, g1, mods_p, prompt_mod, 0, *f1)
    q_p, kT_p, vT_p, kTb_p, vTb_p, lfT_p, cf_p, cfT_p, u_p = _inproj_prompt(
        xp.reshape(bp, s_len, d), gm, mods_p, 3, wq, wkT, wvT, wf, wfT, bf, bfT, wu, page, n_heads, q_scale)
    attn_p = _attn_prompt(q_p, kTb_p, vTb_p, cf_p, cfT_p, head_dim)
    pool_p = _pool_prompt(u_p, wp, ps)
    xp = _outproj(xp, attn_p.reshape(bp * s_len, a), pool_p.reshape(bp * s_len, pw), mods_p, prompt_mod, 5, wo_a, wo_p)
    y_prompt = _ffn(xp, g2, mods_p, prompt_mod, 6, *f2, gf=gf).reshape(bp, s_len, d)

    n_pp = s_len // page
    new_k_prompt = kT_p.reshape(bp, n_pp, n_heads, head_dim, page).transpose(0, 1, 4, 2, 3)[None]
    new_v_prompt = vT_p.reshape(bp, n_pp, n_heads, head_dim, page).transpose(0, 1, 4, 2, 3)[None]
    new_logf_prompt = lfT_p.transpose(0, 1, 3, 2)[None]
    new_pool_prompt = u_p[:, s_len - n_state:][None]

    n_rows = bd * t_len
    xs = x_sample.transpose(1, 0, 2).reshape(n_rows, d)
    xs = _ffn(xs, g1, mods, sample_mod, 0, *f1)
    q_s, k_s, v_s, kT_s, vT_s, lfT_s, c_s, u_s = _inproj_sample(
        xs, gm, mods, sample_mod, 3, wqkv, wkT, wvT, wf, wfT, bf, bfT, wu, t_len, n_heads, q_scale)

    def per_seq(z):
        return z.reshape(t_len, bd, z.shape[-1]).transpose(1, 0, 2)

    q_rep = jnp.repeat(per_seq(q_s), n_heads, axis=1)
    c_bt = per_seq(c_s)
    cq = c_bt.reshape(bd, t_len * n_heads, 1)
    c2 = jnp.broadcast_to(c_bt.transpose(0, 2, 1)[:, None], (bd, t_len, n_heads, t_len)).reshape(bd, t_len * n_heads, t_len)
    cache_kT = cache_k[li].transpose(0, 2, 3, 1).reshape(n_phys, a, page)
    cache_vT = cache_v[li].transpose(0, 2, 3, 1).reshape(n_phys, a, page)
    cache_lfT = cache_logf[li].transpose(0, 2, 1)
    attn_s = _decode(page_table, q_rep, cq, c2, per_seq(k_s), per_seq(v_s), cache_kT, cache_vT, cache_lfT, head_dim)
    attn_s = attn_s.transpose(1, 0, 2).reshape(n_rows, a).astype(BF16)
    state_tm = state_pool[li].transpose(1, 0, 2)
    u_tm = u_s.reshape(t_len, bd, pw)
    pool_s = _pool_sample(state_tm, u_tm, wp, ps).reshape(n_rows, pw)
    xs = _outproj(xs, attn_s, pool_s, mods, sample_mod, 5, wo_a, wo_p)
    y_sample = _ffn(xs, g2, mods, sample_mod, 6, *f2, gf=gf).reshape(t_len, bd, d).transpose(1, 0, 2)

    new_k_sample = kT_s.reshape(t_len, n_heads, head_dim, bd).transpose(3, 0, 1, 2)[None]
    new_v_sample = vT_s.reshape(t_len, n_heads, head_dim, bd).transpose(3, 0, 1, 2)[None]
    new_logf_sample = lfT_s.transpose(2, 0, 1)[None]
    new_pool_sample = jnp.concatenate([state_tm[t_len:], u_tm], axis=0).transpose(1, 0, 2)[None]

    return (y_prompt, y_sample, new_k_prompt, new_v_prompt, new_logf_prompt, new_pool_prompt,
            new_k_sample, new_v_sample, new_logf_sample, new_pool_sample)
```

```python
import functools

import jax
import jax.numpy as jnp
from jax import lax
from jax.experimental import pallas as pl
from jax.experimental.pallas import tpu as pltpu

F32 = jnp.float32
BF16 = jnp.bfloat16

RMS_EPS = 1e-6
POOL_WINDOWS = (2, 4, 8, 16)
N_MODS = 9
LANES = 128
VMEM_LIMIT_BYTES = 56 * 1024 * 1024

ROW_TILE = 512
FFN_CHUNK = 256
ATTN_TILE = 512
DECODE_PAGES_PER_STEP = 16

_NT_DIMS = (((1,), (1,)), ((), ()))


def _dot(a, b):
    return jnp.dot(a, b, preferred_element_type=F32)


def _dot_nt(a, b):
    return lax.dot_general(a, b, _NT_DIMS, preferred_element_type=F32)


def _split3(x):
    hi = x.astype(BF16)
    r = x - hi.astype(F32)
    mid = r.astype(BF16)
    lo = (r - mid.astype(F32)).astype(BF16)
    return hi, mid, lo


def _split3_f32(x):
    hi, mid, lo = _split3(x)
    return hi.astype(F32), mid.astype(F32), lo.astype(F32)


def _dot_exact_left(x, ones_mat):
    hi, mid, lo = _split3(x)
    return _dot(hi, ones_mat) + _dot(mid, ones_mat) + _dot(lo, ones_mat)


def _dot_exact_right(ones_mat, x):
    hi, mid, lo = _split3(x)
    return _dot(ones_mat, hi) + _dot(ones_mat, mid) + _dot(ones_mat, lo)


def _silu(x):
    return x * jax.nn.sigmoid(x)


def _log_sigmoid(x):
    return jnp.minimum(x, 0.0) - jnp.log1p(jnp.exp(-jnp.abs(x)))


def _rmsnorm(x, g):
    return x * lax.rsqrt(jnp.mean(x * x, axis=-1, keepdims=True) + RMS_EPS) * g


def _rows(m, n):
    if m.shape[0] in (1, n):
        return m
    return jnp.concatenate([m] * (n // m.shape[0]), axis=0)


def _modnorm(x, g, shift, scale):
    n = x.shape[0]
    return _rmsnorm(x, g) * (1.0 + _rows(scale, n)) + _rows(shift, n)


def _resident(shape):
    zeros = (0,) * len(shape)
    return pl.BlockSpec(shape, lambda *_: zeros, pipeline_mode=pl.Buffered(1))


def _params():
    return pltpu.CompilerParams(vmem_limit_bytes=VMEM_LIMIT_BYTES)


def _ada_kernel(c_ref, w_ref, b_ref, o_ref):
    s = _silu(c_ref[...]).astype(BF16)
    o_ref[...] = _dot(s, w_ref[...].astype(BF16)) + b_ref[...]


def _ada(c, w, b):
    n, d = c.shape
    m = w.shape[1]
    tn = d
    return pl.pallas_call(
        _ada_kernel,
        grid=(m // tn,),
        in_specs=[
            pl.BlockSpec((n, d), lambda j: (0, 0)),
            pl.BlockSpec((d, tn), lambda j: (0, j)),
            pl.BlockSpec((1, tn), lambda j: (0, j)),
        ],
        out_specs=pl.BlockSpec((n, tn), lambda j: (0, j)),
        out_shape=jax.ShapeDtypeStruct((n, m), F32),
        compiler_params=_params(),
        name="ada_params",
    )(c, w, b)


def _ffn_kernel(x_ref, g_ref, sh_ref, sc_ref, gt_ref, w1_ref, w3_ref, w2_ref, *rest, final_norm):
    if final_norm:
        gf_ref, o_ref, h_ref, acc_ref = rest
    else:
        o_ref, h_ref, acc_ref = rest
    x = x_ref[...]
    n = x.shape[0]
    h_ref[...] = _modnorm(x, g_ref[...], sh_ref[...], sc_ref[...]).astype(BF16)
    acc_ref[...] = jnp.zeros_like(acc_ref)

    def chunk(c, carry):
        off = pl.multiple_of(c * FFN_CHUNK, FFN_CHUNK)
        h = h_ref[...]
        a = _dot(h, w1_ref[:, pl.ds(off, FFN_CHUNK)])
        b = _dot(h, w3_ref[:, pl.ds(off, FFN_CHUNK)])
        act = (_silu(a) * b).astype(BF16)
        acc_ref[...] += _dot(act, w2_ref[pl.ds(off, FFN_CHUNK), :])
        return carry

    lax.fori_loop(0, w1_ref.shape[1] // FFN_CHUNK, chunk, 0)
    out = x + 0.5 * _rows(gt_ref[...], n) * acc_ref[...]
    if final_norm:
        out = _rmsnorm(out, gf_ref[...])
    o_ref[...] = out


def _ffn(x, g, mods, mod_spec, k0, w1, w3, w2, gf=None):
    r, d = x.shape
    dff = w1.shape[1]
    tm = ROW_TILE
    in_specs = [
        pl.BlockSpec((tm, d), lambda i: (i, 0)),
        _resident((1, d)),
        mod_spec(k0),
        mod_spec(k0 + 1),
        mod_spec(k0 + 2),
        _resident((d, dff)),
        _resident((d, dff)),
        _resident((dff, d)),
    ]
    args = [x, g, mods, mods, mods, w1, w3, w2]
    if gf is not None:
        in_specs.append(_resident((1, d)))
        args.append(gf)
    return pl.pallas_call(
        functools.partial(_ffn_kernel, final_norm=gf is not None),
        grid=(r // tm,),
        in_specs=in_specs,
        out_specs=pl.BlockSpec((tm, d), lambda i: (i, 0)),
        out_shape=jax.ShapeDtypeStruct((r, d), F32),
        scratch_shapes=[pltpu.VMEM((tm, d), BF16), pltpu.VMEM((tm, d), F32)],
        compiler_params=_params(),
        name="ffn_final" if gf is not None else "ffn",
    )(*args)


N_BIAS_PARTS = 3


def _inproj_prompt_kernel(x_ref, g_ref, sh_ref, sc_ref, wqT_ref, wk_ref, wkT_ref, wvT_ref, wf_ref, wfT_ref,
                          bf_ref, bfT_ref, wu_ref,
                          qTa_ref, ka_ref, vTb_ref, kT_ref, vT_ref, lfT_ref, u_ref,
                          carry_ref, carryT_ref, *, q_scale, head_dim):
    @pl.when(pl.program_id(1) == 0)
    def _():
        carry_ref[...] = jnp.zeros_like(carry_ref)
        carryT_ref[...] = jnp.zeros_like(carryT_ref)

    tm = x_ref.shape[0]
    page = kT_ref.shape[-1]
    n_heads, depth, _ = qTa_ref.shape
    h = _modnorm(x_ref[...], g_ref[...], sh_ref[...], sc_ref[...]).astype(BF16)
    u_ref[...] = _dot(h, wu_ref[...])
    kT = _dot_nt(wkT_ref[...], h)
    vT = _dot_nt(wvT_ref[...], h)
    vTb_ref[...] = vT.astype(BF16)
    lf = _log_sigmoid(_dot(h, wf_ref[...]) + bf_ref[...])
    lfT = _log_sigmoid(_dot_nt(wfT_ref[...], h) + bfT_ref[...])
    for j in range(tm // page):
        cols = slice(j * page, (j + 1) * page)
        kT_ref[j] = kT[:, cols]
        vT_ref[j] = vT[:, cols]
        lfT_ref[j] = lfT[:, cols]
    r = lax.broadcasted_iota(jnp.int32, (tm, tm), 0)
    c = lax.broadcasted_iota(jnp.int32, (tm, tm), 1)
    cf = _dot_exact_right((c <= r).astype(BF16), lf) + carry_ref[...]
    carry_ref[...] = cf[tm - 1:tm, :]
    cfT = _dot_exact_left(lfT, (r <= c).astype(BF16)) + carryT_ref[:, :1]
    carryT_ref[...] = jnp.broadcast_to(cfT[:, tm - 1:tm], carryT_ref.shape)
    qT = _dot_nt(wqT_ref[...], h) * q_scale
    k = _dot(h, wk_ref[...])
    sub = lax.broadcasted_iota(jnp.int32, (8, tm), 0)
    lane = lax.broadcasted_iota(jnp.int32, (tm, depth), 1)
    pad_rows = jnp.zeros((depth - head_dim - 8, tm), F32)
    for hd in range(n_heads):
        f_hi, f_mid, f_lo = _split3_f32(jnp.broadcast_to(cfT[hd:hd + 1, :], (8, tm)))
        q_bias = jnp.where(sub == 0, f_hi, jnp.where(sub == 1, f_mid, jnp.where(sub == 2, f_lo,
                 jnp.where(sub < 2 * N_BIAS_PARTS, 1.0, 0.0))))
        rows = slice(hd * head_dim, (hd + 1) * head_dim)
        qTa_ref[hd] = jnp.concatenate([qT[rows, :], q_bias, pad_rows], axis=0).astype(BF16)
        n_hi, n_mid, n_lo = _split3_f32(-jnp.broadcast_to(cf[:, hd:hd + 1], (tm, depth)))
        k_bias = jnp.where(lane == head_dim + 3, n_hi, jnp.where(lane == head_dim + 4, n_mid,
                 jnp.where(lane == head_dim + 5, n_lo, jnp.where(lane < head_dim + N_BIAS_PARTS, 1.0, 0.0))))
        blk = k[:, (hd // 2) * depth:(hd // 2 + 1) * depth]
        if hd % 2:
            blk = pltpu.roll(blk, head_dim, 1)
        ka_ref[hd] = jnp.where(lane < head_dim, blk, k_bias).astype(BF16)


def _inproj_prompt(x, g, mods, k0, wqT, wk, wkT, wvT, wf, wfT, bf, bfT, wu, page, n_heads, head_dim, q_scale):
    b, s, d = x.shape
    a = wk.shape[1]
    pw = wu.shape[1]
    tm = ROW_TILE
    npg = tm // page
    depth = 2 * head_dim
    assert depth == LANES and depth - head_dim >= 8

    def mod_spec(k):
        return pl.BlockSpec((None, 1, d), lambda bi, ti: (bi, 0, k))

    out_shape = (
        jax.ShapeDtypeStruct((b, n_heads, depth, s), BF16),
        jax.ShapeDtypeStruct((b, n_heads, s, depth), BF16),
        jax.ShapeDtypeStruct((b, a, s), BF16),
        jax.ShapeDtypeStruct((b, s // page, a, page), F32),
        jax.ShapeDtypeStruct((b, s // page, a, page), F32),
        jax.ShapeDtypeStruct((b, s // page, n_heads, page), F32),
        jax.ShapeDtypeStruct((b, s, pw), F32),
    )
    out_specs = (
        pl.BlockSpec((None, n_heads, depth, tm), lambda bi, ti: (bi, 0, 0, ti)),
        pl.BlockSpec((None, n_heads, tm, depth), lambda bi, ti: (bi, 0, ti, 0)),
        pl.BlockSpec((None, a, tm), lambda bi, ti: (bi, 0, ti)),
        pl.BlockSpec((None, npg, a, page), lambda bi, ti: (bi, ti, 0, 0)),
        pl.BlockSpec((None, npg, a, page), lambda bi, ti: (bi, ti, 0, 0)),
        pl.BlockSpec((None, npg, n_heads, page), lambda bi, ti: (bi, ti, 0, 0)),
        pl.BlockSpec((None, tm, pw), lambda bi, ti: (bi, ti, 0)),
    )
    weights = (wqT, wk, wkT, wvT, wf, wfT, bf, bfT, wu)
    in_specs = [
        pl.BlockSpec((None, tm, d), lambda bi, ti: (bi, ti, 0)),
        _resident((1, d)),
        mod_spec(k0),
        mod_spec(k0 + 1),
    ] + [_resident(w.shape) for w in weights]
    return pl.pallas_call(
        functools.partial(_inproj_prompt_kernel, q_scale=q_scale, head_dim=head_dim),
        grid=(b, s // tm),
        in_specs=in_specs,
        out_specs=out_specs,
        out_shape=out_shape,
        scratch_shapes=[pltpu.VMEM((1, LANES), F32), pltpu.VMEM((n_heads, LANES), F32)],
        compiler_params=_params(),
        name="inproj_prompt",
    )(x, g, mods, mods, *weights)


def _attn_prompt_kernel(qTa_ref, ka_ref, vT_ref, o_ref, m_ref, l_ref, acc_ref, *, head_dim):
    qi = pl.program_id(2)
    tq = qTa_ref.shape[-1]
    tk = tq
    m_ref[...] = jnp.full_like(m_ref, -jnp.inf)
    l_ref[...] = jnp.zeros_like(l_ref)
    acc_ref[...] = jnp.zeros_like(acc_ref)

    def step(j, masked):
        off = pl.multiple_of(j * tk, tk)
        scores = [_dot(ka_ref[i, pl.ds(off, tk), :], qTa_ref[i]) for i in range(2)]
        for i in range(2):
            s = scores[i]
            if masked:
                key = lax.broadcasted_iota(jnp.int32, (tk, tq), 0)
                qry = lax.broadcasted_iota(jnp.int32, (tk, tq), 1)
                s = jnp.where(key <= qry, s, -jnp.inf)
            m_prev = m_ref[i]
            m_new = jnp.maximum(m_prev, jnp.max(s, axis=0, keepdims=True))
            alpha = jnp.exp(m_prev - m_new)
            p = jnp.exp(s - m_new)
            l_ref[i] = alpha * l_ref[i] + jnp.sum(p, axis=0, keepdims=True)
            vT = vT_ref[pl.ds(i * head_dim, head_dim), pl.ds(off, tk)]
            acc_ref[i] = alpha * acc_ref[i] + _dot(vT, p.astype(BF16))
            m_ref[i] = m_new

    def full_block(j, carry):
        step(j, False)
        return carry

    lax.fori_loop(0, qi, full_block, 0)
    step(qi, True)
    oT = jnp.concatenate([acc_ref[i] / l_ref[i] for i in range(2)], axis=0)
    o_ref[...] = oT.T.astype(BF16)


def _attn_prompt(qTa, ka, vTb, head_dim):
    b, n_heads, depth, s = qTa.shape
    a = vTb.shape[1]
    tq = ATTN_TILE
    return pl.pallas_call(
        functools.partial(_attn_prompt_kernel, head_dim=head_dim),
        grid=(b, n_heads // 2, s // tq),
        in_specs=[
            pl.BlockSpec((None, 2, depth, tq), lambda bi, hp, qi: (bi, hp, 0, qi)),
            pl.BlockSpec((None, 2, s, depth), lambda bi, hp, qi: (bi, hp, 0, 0)),
            pl.BlockSpec((None, depth, s), lambda bi, hp, qi: (bi, hp, 0)),
        ],
        out_specs=pl.BlockSpec((None, tq, depth), lambda bi, hp, qi: (bi, qi, hp)),
        out_shape=jax.ShapeDtypeStruct((b, s, a), BF16),
        scratch_shapes=[
            pltpu.VMEM((2, 1, tq), F32),
            pltpu.VMEM((2, 1, tq), F32),
            pltpu.VMEM((2, head_dim, tq), F32),
        ],
        compiler_params=_params(),
        name="attn_prompt",
    )(qTa, ka, vTb)


def _pool_prompt_kernel(u_ref, wp_ref, ps_ref, o_ref, halo_ref):
    ti = pl.program_id(1)

    @pl.when(ti == 0)
    def _():
        halo_ref[...] = jnp.zeros_like(halo_ref)

    tm = u_ref.shape[0]
    nh = halo_ref.shape[0]
    gd = wp_ref.shape[1]
    u = u_ref[...]
    ext = jnp.concatenate([halo_ref[...], u], axis=0)
    halo_ref[...] = u[tm - nh:, :]
    sums = {1: ext}
    w = 1
    while w < max(POOL_WINDOWS):
        sums[2 * w] = sums[w] + pltpu.roll(sums[w], w, 0)
        w *= 2
    pos = ti * tm + lax.broadcasted_iota(jnp.int32, (tm, 1), 0)
    ys = []
    for gi, w in enumerate(POOL_WINDOWS):
        lanes = slice(gi * gd, (gi + 1) * gd)
        count = jnp.minimum(pos + 1, w).astype(F32)
        d = sums[w][nh:, lanes] / count - u[:, lanes]
        ys.append(_dot(d.astype(BF16), wp_ref[gi]))
    o_ref[...] = (jnp.concatenate(ys, axis=1) * ps_ref[...]).astype(BF16)


def _pool_prompt(u, wp, ps):
    b, s, pw = u.shape
    tm = ROW_TILE
    halo = 16
    assert halo >= max(POOL_WINDOWS)
    return pl.pallas_call(
        _pool_prompt_kernel,
        grid=(b, s // tm),
        in_specs=[
            pl.BlockSpec((None, tm, pw), lambda bi, ti: (bi, ti, 0)),
            _resident(wp.shape),
            _resident(ps.shape),
        ],
        out_specs=pl.BlockSpec((None, tm, pw), lambda bi, ti: (bi, ti, 0)),
        out_shape=jax.ShapeDtypeStruct((b, s, pw), BF16),
        scratch_shapes=[pltpu.VMEM((halo, pw), F32)],
        compiler_params=_params(),
        name="pool_prompt",
    )(u, wp, ps)


def _outproj_kernel(x_ref, a_ref, p_ref, gt_ref, wa_ref, wp_ref, o_ref):
    x = x_ref[...]
    mix = _dot(a_ref[...], wa_ref[...]) + _dot(p_ref[...], wp_ref[...])
    o_ref[...] = x + _rows(gt_ref[...], x.shape[0]) * mix


def _outproj(x, attn, pool, mods, mod_spec, k, wa, wp):
    r, d = x.shape
    tm = ROW_TILE
    return pl.pallas_call(
        _outproj_kernel,
        grid=(r // tm,),
        in_specs=[
            pl.BlockSpec((tm, d), lambda i: (i, 0)),
            pl.BlockSpec((tm, attn.shape[1]), lambda i: (i, 0)),
            pl.BlockSpec((tm, pool.shape[1]), lambda i: (i, 0)),
            mod_spec(k),
            _resident(wa.shape),
            _resident(wp.shape),
        ],
        out_specs=pl.BlockSpec((tm, d), lambda i: (i, 0)),
        out_shape=jax.ShapeDtypeStruct((r, d), F32),
        compiler_params=_params(),
        name="outproj",
    )(x, attn, pool, mods, wa, wp)


def _inproj_sample_kernel(x_ref, g_ref, sh_ref, sc_ref, wqkv_ref, wkT_ref, wvT_ref, wf_ref, wfT_ref,
                          bf_ref, bfT_ref, wu_ref,
                          q_ref, k_ref, v_ref, kT_ref, vT_ref, lfT_ref, c_ref, u_ref, *, q_scale):
    n_t, _, nb = kT_ref.shape
    n_heads = c_ref.shape[-1]
    h = _modnorm(x_ref[...], g_ref[...], sh_ref[...], sc_ref[...]).astype(BF16)
    z = _dot(h, wqkv_ref[...])
    a = z.shape[1] // 3
    q_ref[...] = (z[:, :a] * q_scale).astype(BF16)
    k_ref[...] = z[:, a:2 * a]
    v_ref[...] = z[:, 2 * a:]
    u_ref[...] = _dot(h, wu_ref[...])
    kT = _dot_nt(wkT_ref[...], h)
    vT = _dot_nt(wvT_ref[...], h)
    lf = _log_sigmoid(_dot(h, wf_ref[...]) + bf_ref[...])
    lfT = _log_sigmoid(_dot_nt(wfT_ref[...], h) + bfT_ref[...])
    run = None
    parts = []
    for t in range(n_t):
        cols = slice(t * nb, (t + 1) * nb)
        kT_ref[t] = kT[:, cols]
        vT_ref[t] = vT[:, cols]
        lfT_ref[t] = lfT[:, cols]
        blk = lf[t * nb:(t + 1) * nb, :]
        run = blk if run is None else run + blk
        parts.append(run)
    c_ref[...] = jnp.concatenate(parts, axis=0)[:, :n_heads]


def _inproj_sample(x, g, mods, mod_spec, k0, wqkv, wkT, wvT, wf, wfT, bf, bfT, wu, n_t, n_heads, q_scale):
    r, d = x.shape
    a = wkT.shape[0]
    pw = wu.shape[1]
    nb = r // n_t
    out_shape = (
        jax.ShapeDtypeStruct((r, a), BF16),
        jax.ShapeDtypeStruct((r, a), F32),
        jax.ShapeDtypeStruct((r, a), F32),
        jax.ShapeDtypeStruct((n_t, a, nb), F32),
        jax.ShapeDtypeStruct((n_t, a, nb), F32),
        jax.ShapeDtypeStruct((n_t, n_heads, nb), F32),
        jax.ShapeDtypeStruct((r, n_heads), F32),
        jax.ShapeDtypeStruct((r, pw), F32),
    )
    whole = lambda shp: pl.BlockSpec(shp, lambda i: (0,) * len(shp))
    in_specs = [whole((r, d)), whole((1, d)), mod_spec(k0), mod_spec(k0 + 1)] + [
        whole(w.shape) for w in (wqkv, wkT, wvT, wf, wfT, bf, bfT, wu)
    ]
    return pl.pallas_call(
        functools.partial(_inproj_sample_kernel, q_scale=q_scale),
        grid=(1,),
        in_specs=in_specs,
        out_specs=tuple(whole(o.shape) for o in out_shape),
        out_shape=out_shape,
        compiler_params=_params(),
        name="inproj_sample",
    )(x, g, mods, mods, wqkv, wkT, wvT, wf, wfT, bf, bfT, wu)


def _decode_kernel(pt_ref, q_ref, cq_ref, c2_ref, kn_ref, vn_ref, *rest, pages_per_step, head_dim):
    del pt_ref
    g_n = pages_per_step
    k_refs = rest[:g_n]
    v_refs = rest[g_n:2 * g_n]
    lf_refs = rest[2 * g_n:3 * g_n]
    o_ref, m_ref, l_ref, acc_ref, r_ref = rest[3 * g_n:]
    step = pl.program_id(1)
    n_rows, width = q_ref.shape
    n_heads = width // head_dim
    n_t = n_rows // n_heads
    row = lax.broadcasted_iota(jnp.int32, (n_rows, 1), 0)
    lane = lax.broadcasted_iota(jnp.int32, (1, width), 1)
    own = (lane // head_dim) == (row % n_heads)
    t_row = row // n_heads
    q = q_ref[...]
    qbd = jnp.where(own, q, jnp.zeros_like(q))
    cq = cq_ref[...]

    @pl.when(step == 0)
    def _():
        qf = qbd.astype(F32)
        kn = kn_ref[...]
        vn = vn_ref[...]
        c2 = c2_ref[...]
        sc = []
        for s in range(n_t):
            col = jnp.sum(qf * kn[s:s + 1, :], axis=1, keepdims=True) + cq - c2[:, s:s + 1]
            sc.append(jnp.where(t_row >= s, col, -jnp.inf))
        m0 = functools.reduce(jnp.maximum, sc)
        l0 = jnp.zeros_like(m0)
        a0 = jnp.zeros(acc_ref.shape, F32)
        for s in range(n_t):
            p = jnp.exp(sc[s] - m0)
            l0 = l0 + p
            a0 = a0 + p * vn[s:s + 1, :]
        m_ref[...] = m0
        l_ref[...] = l0
        acc_ref[...] = a0
        r_ref[...] = jnp.zeros_like(r_ref)

    page = lf_refs[0].shape[-1]
    pj = lax.broadcasted_iota(jnp.int32, (page, page), 0)
    ps = lax.broadcasted_iota(jnp.int32, (page, page), 1)
    later = (pj > ps).astype(BF16)
    r_run = r_ref[...]
    lf_all = jnp.concatenate([lf_refs[i][...] for i in range(g_n)], axis=0)
    suffix_all = _dot_exact_left(lf_all, later)
    totals = jnp.sum(lf_all, axis=1, keepdims=True)
    scores = []
    for i in range(g_n):
        rows = slice(i * n_heads, (i + 1) * n_heads)
        suffix = suffix_all[rows] + r_run
        r_run = r_run + totals[rows]
        bias = jnp.concatenate([suffix] * n_t, axis=0) + cq
        scores.append(_dot(qbd, k_refs[i][...].astype(BF16)) + bias)
    r_ref[...] = r_run
    m_prev = m_ref[...]
    m_new = functools.reduce(jnp.maximum, [m_prev] + [jnp.max(s, axis=1, keepdims=True) for s in scores])
    alpha = jnp.exp(m_prev - m_new)
    l_new = alpha * l_ref[...]
    acc = alpha * acc_ref[...]
    for i in range(g_n):
        p = jnp.exp(scores[i] - m_new)
        l_new = l_new + jnp.sum(p, axis=1, keepdims=True)
        acc = acc + _dot_nt(p.astype(BF16), v_refs[i][...].astype(BF16))
    m_ref[...] = m_new
    l_ref[...] = l_new
    acc_ref[...] = acc

    @pl.when(step == pl.num_programs(1) - 1)
    def _():
        o = jnp.where(own, acc, 0.0) / l_new
        o_ref[...] = o.reshape(n_t, n_heads, width).sum(axis=1)


def _decode(page_table, q_rep, cq, c2, k_new, v_new, cache_kT, cache_vT, cache_lfT, head_dim):
    bd, n_pages = page_table.shape
    _, n_rows, width = q_rep.shape
    n_t = k_new.shape[1]
    page = cache_kT.shape[-1]
    n_heads = cache_lfT.shape[1]
    g_n = DECODE_PAGES_PER_STEP

    def page_spec(shape, i):
        def index(b, g, pt):
            return (pt[b, n_pages - 1 - (g * g_n + i)], 0, 0)
        return pl.BlockSpec((None,) + shape, index)

    per_seq = lambda shp: pl.BlockSpec((None,) + shp, lambda b, g, pt: (b, 0, 0))
    in_specs = [
        per_seq((n_rows, width)),
        per_seq((n_rows, 1)),
        per_seq((n_rows, n_t)),
        per_seq((n_t, width)),
        per_seq((n_t, width)),
    ]
    in_specs += [page_spec((width, page), i) for i in range(g_n)]
    in_specs += [page_spec((width, page), i) for i in range(g_n)]
    in_specs += [page_spec((n_heads, page), i) for i in range(g_n)]
    grid_spec = pltpu.PrefetchScalarGridSpec(
        num_scalar_prefetch=1,
        grid=(bd, n_pages // g_n),
        in_specs=in_specs,
        out_specs=per_seq((n_t, width)),
        scratch_shapes=[
            pltpu.VMEM((n_rows, 1), F32),
            pltpu.VMEM((n_rows, 1), F32),
            pltpu.VMEM((n_rows, width), F32),
            pltpu.VMEM((n_heads, LANES), F32),
        ],
    )
    return pl.pallas_call(
        functools.partial(_decode_kernel, pages_per_step=g_n, head_dim=head_dim),
        grid_spec=grid_spec,
        out_shape=jax.ShapeDtypeStruct((bd, n_t, width), F32),
        compiler_params=_params(),
        name="decode_attn",
    )(page_table, q_rep, cq, c2, k_new, v_new, *([cache_kT] * g_n), *([cache_vT] * g_n), *([cache_lfT] * g_n))


def _pool_sample_kernel(st_ref, u_ref, wp_ref, ps_ref, o_ref):
    n_state = st_ref.shape[0]
    n_t = u_ref.shape[0]
    gd = wp_ref.shape[1]

    def rows(idx, lanes):
        return st_ref[idx, :, lanes] if idx < n_state else u_ref[idx - n_state, :, lanes]

    for t in range(n_t):
        ys = []
        for gi, w in enumerate(POOL_WINDOWS):
            lanes = slice(gi * gd, (gi + 1) * gd)
            cur = rows(n_state + t, lanes)
            total = cur
            for j in range(1, w):
                total = total + rows(n_state + t - j, lanes)
            d = total / float(w) - cur
            ys.append(_dot(d.astype(BF16), wp_ref[gi]))
        o_ref[t] = (jnp.concatenate(ys, axis=1) * ps_ref[...]).astype(BF16)


def _pool_sample(state_tm, u_tm, wp, ps):
    assert state_tm.shape[0] >= max(POOL_WINDOWS) - 1
    whole = lambda shp: pl.BlockSpec(shp, lambda i: (0,) * len(shp))
    return pl.pallas_call(
        _pool_sample_kernel,
        grid=(1,),
        in_specs=[whole(state_tm.shape), whole(u_tm.shape), whole(wp.shape), whole(ps.shape)],
        out_specs=whole(u_tm.shape),
        out_shape=jax.ShapeDtypeStruct(u_tm.shape, BF16),
        compiler_params=_params(),
        name="pool_sample",
    )(state_tm, u_tm, wp, ps)


def kernel(x_prompt, x_sample, cache_k, cache_v, cache_logf, state_pool, page_table, c_prompt, c_sample, norm_ffn1, norm_mix, norm_ffn2, norm_final, w_ada, b_ada, ffn1_w1, ffn1_w3, ffn1_w2, ffn2_w1, ffn2_w3, ffn2_w2, w_in, b_f, w_o, w_pool, pool_scale):
    bp, s_len, d = x_prompt.shape
    bd, t_len, _ = x_sample.shape
    depth, n_phys, page, n_heads, head_dim = cache_k.shape
    assert depth == 1, "single-layer model"
    a = n_heads * head_dim
    pw = state_pool.shape[-1]
    n_state = state_pool.shape[2]
    q_scale = head_dim ** -0.5
    li = 0

    n_pad = (-(bd + bp)) % 16
    c_all = jnp.concatenate([c_sample, c_prompt, jnp.zeros((n_pad, d), F32)], axis=0)
    mods = _ada(c_all, w_ada[li], b_ada[li][None, :])
    mods_p = mods[bd:bd + bp].reshape(bp, 1, N_MODS * d)
    tiles_per_seq = s_len // ROW_TILE

    def prompt_mod(k):
        return pl.BlockSpec((None, 1, d), lambda i: (i // tiles_per_seq, 0, k))

    def sample_mod(k):
        return pl.BlockSpec((bd, d), lambda i: (0, k))

    f1 = [w[li].astype(BF16) for w in (ffn1_w1, ffn1_w3, ffn1_w2)]
    f2 = [w[li].astype(BF16) for w in (ffn2_w1, ffn2_w3, ffn2_w2)]
    w_in_l = w_in[li]
    wqkv = w_in_l[:, :3 * a].astype(BF16)
    wqT = w_in_l[:, :a].T.astype(BF16)
    wk = w_in_l[:, a:2 * a].astype(BF16)
    wkT = w_in_l[:, a:2 * a].T.astype(BF16)
    wvT = w_in_l[:, 2 * a:3 * a].T.astype(BF16)
    wf_cols = w_in_l[:, 3 * a:3 * a + n_heads]
    wf = jnp.pad(wf_cols, ((0, 0), (0, LANES - n_heads))).astype(BF16)
    wfT = wf_cols.T.astype(BF16)
    bf = jnp.pad(b_f[li][None, :], ((0, 0), (0, LANES - n_heads)))
    bfT = b_f[li][:, None]
    wu = w_in_l[:, 3 * a + n_heads:].astype(BF16)
    wo_a = w_o[li][:a].astype(BF16)
    wo_p = w_o[li][a:].astype(BF16)
    wp = w_pool[li].astype(BF16)
    ps = pool_scale[li][None, :]
    g1 = norm_ffn1[li][None, :]
    gm = norm_mix[li][None, :]
    g2 = norm_ffn2[li][None, :]
    gf = norm_final[None, :]

    xp = x_prompt.reshape(bp * s_len, d)
    xp = _ffn(xp, g1, mods_p, prompt_mod, 0, *f1)
    qTa_p, ka_p, vTb_p, kT_p, vT_p, lfT_p, u_p = _inproj_prompt(
        xp.reshape(bp, s_len, d), gm, mods_p, 3, wqT, wk, wkT, wvT, wf, wfT, bf, bfT, wu,
        page, n_heads, head_dim, q_scale)
    attn_p = _attn_prompt(qTa_p, ka_p, vTb_p, head_dim)
    pool_p = _pool_prompt(u_p, wp, ps)
    xp = _outproj(xp, attn_p.reshape(bp * s_len, a), pool_p.reshape(bp * s_len, pw), mods_p, prompt_mod, 5, wo_a, wo_p)
    y_prompt = _ffn(xp, g2, mods_p, prompt_mod, 6, *f2, gf=gf).reshape(bp, s_len, d)

    n_pp = s_len // page
    new_k_prompt = kT_p.reshape(bp, n_pp, n_heads, head_dim, page).transpose(0, 1, 4, 2, 3)[None]
    new_v_prompt = vT_p.reshape(bp, n_pp, n_heads, head_dim, page).transpose(0, 1, 4, 2, 3)[None]
    new_logf_prompt = lfT_p.transpose(0, 1, 3, 2)[None]
    new_pool_prompt = u_p[:, s_len - n_state:][None]

    n_rows = bd * t_len
    xs = x_sample.transpose(1, 0, 2).reshape(n_rows, d)
    xs = _ffn(xs, g1, mods, sample_mod, 0, *f1)
    q_s, k_s, v_s, kT_s, vT_s, lfT_s, c_s, u_s = _inproj_sample(
        xs, gm, mods, sample_mod, 3, wqkv, wkT, wvT, wf, wfT, bf, bfT, wu, t_len, n_heads, q_scale)

    def per_seq(z):
        return z.reshape(t_len, bd, z.shape[-1]).transpose(1, 0, 2)

    q_rep = jnp.repeat(per_seq(q_s), n_heads, axis=1)
    c_bt = per_seq(c_s)
    cq = c_bt.reshape(bd, t_len * n_heads, 1)
    c2 = jnp.broadcast_to(c_bt.transpose(0, 2, 1)[:, None], (bd, t_len, n_heads, t_len)).reshape(bd, t_len * n_heads, t_len)
    cache_kT = cache_k[li].transpose(0, 2, 3, 1).reshape(n_phys, a, page)
    cache_vT = cache_v[li].transpose(0, 2, 3, 1).reshape(n_phys, a, page)
    cache_lfT = cache_logf[li].transpose(0, 2, 1)
    attn_s = _decode(page_table, q_rep, cq, c2, per_seq(k_s), per_seq(v_s), cache_kT, cache_vT, cache_lfT, head_dim)
    attn_s = attn_s.transpose(1, 0, 2).reshape(n_rows, a).astype(BF16)
    state_tm = state_pool[li].transpose(1, 0, 2)
    u_tm = u_s.reshape(t_len, bd, pw)
    pool_s = _pool_sample(state_tm, u_tm, wp, ps).reshape(n_rows, pw)
    xs = _outproj(xs, attn_s, pool_s, mods, sample_mod, 5, wo_a, wo_p)
    y_sample = _ffn(xs, g2, mods, sample_mod, 6, *f2, gf=gf).reshape(t_len, bd, d).transpose(1, 0, 2)

    new_k_sample = kT_s.reshape(t_len, n_heads, head_dim, bd).transpose(3, 0, 1, 2)[None]
    new_v_sample = vT_s.reshape(t_len, n_heads, head_dim, bd).transpose(3, 0, 1, 2)[None]
    new_logf_sample = lfT_s.transpose(2, 0, 1)[None]
    new_pool_sample = jnp.concatenate([state_tm[t_len:], u_tm], axis=0).transpose(1, 0, 2)[None]

    return (y_prompt, y_sample, new_k_prompt, new_v_prompt, new_logf_prompt, new_pool_prompt,
            new_k_sample, new_v_sample, new_logf_sample, new_pool_sample)
```

```python
import functools

import jax
import jax.numpy as jnp
from jax import lax
from jax.experimental import pallas as pl
from jax.experimental.pallas import tpu as pltpu

F32 = jnp.float32
BF16 = jnp.bfloat16

RMS_EPS = 1e-6
POOL_WINDOWS = (2, 4, 8, 16)
N_MODS = 9
LANES = 128
VMEM_LIMIT_BYTES = 56 * 1024 * 1024

ROW_TILE = 512
FFN_CHUNK = 256
ATTN_TILE = 512
DECODE_PAGES_PER_STEP = 16

_NT_DIMS = (((1,), (1,)), ((), ()))


def _dot(a, b):
    return jnp.dot(a, b, preferred_element_type=F32)


def _dot_nt(a, b):
    return lax.dot_general(a, b, _NT_DIMS, preferred_element_type=F32)


def _split3(x):
    hi = x.astype(BF16)
    r = x - hi.astype(F32)
    mid = r.astype(BF16)
    lo = (r - mid.astype(F32)).astype(BF16)
    return hi, mid, lo


def _split3_f32(x):
    hi, mid, lo = _split3(x)
    return hi.astype(F32), mid.astype(F32), lo.astype(F32)


def _dot_exact_left(x, ones_mat):
    hi, mid, lo = _split3(x)
    return _dot(hi, ones_mat) + _dot(mid, ones_mat) + _dot(lo, ones_mat)


def _dot_exact_right(ones_mat, x):
    hi, mid, lo = _split3(x)
    return _dot(ones_mat, hi) + _dot(ones_mat, mid) + _dot(ones_mat, lo)


def _silu(x):
    return x * jax.nn.sigmoid(x)


def _log_sigmoid(x):
    return jnp.minimum(x, 0.0) - jnp.log1p(jnp.exp(-jnp.abs(x)))


def _rmsnorm(x, g):
    return x * lax.rsqrt(jnp.mean(x * x, axis=-1, keepdims=True) + RMS_EPS) * g


def _rows(m, n):
    if m.shape[0] in (1, n):
        return m
    return jnp.concatenate([m] * (n // m.shape[0]), axis=0)


def _modnorm(x, g, shift, scale):
    n = x.shape[0]
    return _rmsnorm(x, g) * (1.0 + _rows(scale, n)) + _rows(shift, n)


def _resident(shape):
    zeros = (0,) * len(shape)
    return pl.BlockSpec(shape, lambda *_: zeros, pipeline_mode=pl.Buffered(1))


def _params():
    return pltpu.CompilerParams(vmem_limit_bytes=VMEM_LIMIT_BYTES)


def _ada_kernel(c_ref, w_ref, b_ref, o_ref):
    s = _silu(c_ref[...]).astype(BF16)
    o_ref[...] = _dot(s, w_ref[...].astype(BF16)) + b_ref[...]


def _ada(c, w, b):
    n, d = c.shape
    m = w.shape[1]
    tn = d
    return pl.pallas_call(
        _ada_kernel,
        grid=(m // tn,),
        in_specs=[
            pl.BlockSpec((n, d), lambda j: (0, 0)),
            pl.BlockSpec((d, tn), lambda j: (0, j)),
            pl.BlockSpec((1, tn), lambda j: (0, j)),
        ],
        out_specs=pl.BlockSpec((n, tn), lambda j: (0, j)),
        out_shape=jax.ShapeDtypeStruct((n, m), F32),
        compiler_params=_params(),
        name="ada_params",
    )(c, w, b)


def _ffn_kernel(x_ref, g_ref, sh_ref, sc_ref, gt_ref, w1_ref, w3_ref, w2_ref, *rest, final_norm):
    if final_norm:
        gf_ref, o_ref, h_ref, acc_ref = rest
    else:
        o_ref, h_ref, acc_ref = rest
    x = x_ref[...]
    n = x.shape[0]
    h_ref[...] = _modnorm(x, g_ref[...], sh_ref[...], sc_ref[...]).astype(BF16)
    acc_ref[...] = jnp.zeros_like(acc_ref)

    def chunk(c, carry):
        off = pl.multiple_of(c * FFN_CHUNK, FFN_CHUNK)
        h = h_ref[...]
        a = _dot(h, w1_ref[:, pl.ds(off, FFN_CHUNK)])
        b = _dot(h, w3_ref[:, pl.ds(off, FFN_CHUNK)])
        act = (_silu(a) * b).astype(BF16)
        acc_ref[...] += _dot(act, w2_ref[pl.ds(off, FFN_CHUNK), :])
        return carry

    lax.fori_loop(0, w1_ref.shape[1] // FFN_CHUNK, chunk, 0, unroll=True)
    out = x + 0.5 * _rows(gt_ref[...], n) * acc_ref[...]
    if final_norm:
        out = _rmsnorm(out, gf_ref[...])
    o_ref[...] = out


def _ffn(x, g, mods, mod_spec, k0, w1, w3, w2, gf=None):
    r, d = x.shape
    dff = w1.shape[1]
    tm = ROW_TILE
    in_specs = [
        pl.BlockSpec((tm, d), lambda i: (i, 0)),
        _resident((1, d)),
        mod_spec(k0),
        mod_spec(k0 + 1),
        mod_spec(k0 + 2),
        _resident((d, dff)),
        _resident((d, dff)),
        _resident((dff, d)),
    ]
    args = [x, g, mods, mods, mods, w1, w3, w2]
    if gf is not None:
        in_specs.append(_resident((1, d)))
        args.append(gf)
    return pl.pallas_call(
        functools.partial(_ffn_kernel, final_norm=gf is not None),
        grid=(r // tm,),
        in_specs=in_specs,
        out_specs=pl.BlockSpec((tm, d), lambda i: (i, 0)),
        out_shape=jax.ShapeDtypeStruct((r, d), F32),
        scratch_shapes=[pltpu.VMEM((tm, d), BF16), pltpu.VMEM((tm, d), F32)],
        compiler_params=_params(),
        name="ffn_final" if gf is not None else "ffn",
    )(*args)


N_BIAS_PARTS = 3
LOG2_E = 1.4426950408889634
PV_SUM_ROWS = 16


def _inproj_prompt_kernel(x_ref, g_ref, sh_ref, sc_ref, wqT_ref, wk_ref, wkT_ref, wvT_ref, wf_ref, wfT_ref,
                          bf_ref, bfT_ref, wu_ref,
                          qTa_ref, ka_ref, vTa_ref, kT_ref, vT_ref, lfT_ref, u_ref,
                          carry_ref, carryT_ref, *, q_scale, head_dim):
    @pl.when(pl.program_id(1) == 0)
    def _():
        carry_ref[...] = jnp.zeros_like(carry_ref)
        carryT_ref[...] = jnp.zeros_like(carryT_ref)

    tm = x_ref.shape[0]
    page = kT_ref.shape[-1]
    n_heads, depth, _ = qTa_ref.shape
    h = _modnorm(x_ref[...], g_ref[...], sh_ref[...], sc_ref[...]).astype(BF16)
    u_ref[...] = _dot(h, wu_ref[...])
    kT = _dot_nt(wkT_ref[...], h)
    vT = _dot_nt(wvT_ref[...], h)
    lf = _log_sigmoid(_dot(h, wf_ref[...]) + bf_ref[...])
    lfT = _log_sigmoid(_dot_nt(wfT_ref[...], h) + bfT_ref[...])
    for j in range(tm // page):
        cols = slice(j * page, (j + 1) * page)
        kT_ref[j] = kT[:, cols]
        vT_ref[j] = vT[:, cols]
        lfT_ref[j] = lfT[:, cols]
    r = lax.broadcasted_iota(jnp.int32, (tm, tm), 0)
    c = lax.broadcasted_iota(jnp.int32, (tm, tm), 1)
    cf = _dot_exact_right((c <= r).astype(BF16), lf) + carry_ref[...]
    carry_ref[...] = cf[tm - 1:tm, :]
    cfT = _dot_exact_left(lfT, (r <= c).astype(BF16)) + carryT_ref[:, :1]
    carryT_ref[...] = jnp.broadcast_to(cfT[:, tm - 1:tm], carryT_ref.shape)
    qT = _dot_nt(wqT_ref[...], h) * (q_scale * LOG2_E)
    k = _dot(h, wk_ref[...])
    sub = lax.broadcasted_iota(jnp.int32, (8, tm), 0)
    lane = lax.broadcasted_iota(jnp.int32, (tm, depth), 1)
    pad_rows = jnp.zeros((depth - head_dim - 8, tm), F32)
    sum_rows = vTa_ref.shape[1] - head_dim
    ones_row = jnp.where(lax.broadcasted_iota(jnp.int32, (sum_rows, tm), 0) == 0, 1.0, 0.0)
    for hd in range(n_heads):
        rows = slice(hd * head_dim, (hd + 1) * head_dim)
        vTa_ref[hd] = jnp.concatenate([vT[rows, :], ones_row], axis=0).astype(BF16)
        f_hi, f_mid, f_lo = _split3_f32(jnp.broadcast_to(cfT[hd:hd + 1, :] * LOG2_E, (8, tm)))
        q_bias = jnp.where(sub == 0, f_hi, jnp.where(sub == 1, f_mid, jnp.where(sub == 2, f_lo,
                 jnp.where(sub < 2 * N_BIAS_PARTS, 1.0, 0.0))))
        qTa_ref[hd] = jnp.concatenate([qT[rows, :], q_bias, pad_rows], axis=0).astype(BF16)
        n_hi, n_mid, n_lo = _split3_f32(jnp.broadcast_to(cf[:, hd:hd + 1] * -LOG2_E, (tm, depth)))
        k_bias = jnp.where(lane == head_dim + 3, n_hi, jnp.where(lane == head_dim + 4, n_mid,
                 jnp.where(lane == head_dim + 5, n_lo, jnp.where(lane < head_dim + N_BIAS_PARTS, 1.0, 0.0))))
        blk = k[:, (hd // 2) * depth:(hd // 2 + 1) * depth]
        if hd % 2:
            blk = pltpu.roll(blk, head_dim, 1)
        ka_ref[hd] = jnp.where(lane < head_dim, blk, k_bias).astype(BF16)


def _inproj_prompt(x, g, mods, k0, wqT, wk, wkT, wvT, wf, wfT, bf, bfT, wu, page, n_heads, head_dim, q_scale):
    b, s, d = x.shape
    a = wk.shape[1]
    pw = wu.shape[1]
    tm = ROW_TILE
    npg = tm // page
    depth = 2 * head_dim
    assert depth == LANES and depth - head_dim >= 8

    def mod_spec(k):
        return pl.BlockSpec((None, 1, d), lambda bi, ti: (bi, 0, k))

    out_shape = (
        jax.ShapeDtypeStruct((b, n_heads, depth, s), BF16),
        jax.ShapeDtypeStruct((b, n_heads, s, depth), BF16),
        jax.ShapeDtypeStruct((b, n_heads, head_dim + PV_SUM_ROWS, s), BF16),
        jax.ShapeDtypeStruct((b, s // page, a, page), F32),
        jax.ShapeDtypeStruct((b, s // page, a, page), F32),
        jax.ShapeDtypeStruct((b, s // page, n_heads, page), F32),
        jax.ShapeDtypeStruct((b, s, pw), F32),
    )
    out_specs = (
        pl.BlockSpec((None, n_heads, depth, tm), lambda bi, ti: (bi, 0, 0, ti)),
        pl.BlockSpec((None, n_heads, tm, depth), lambda bi, ti: (bi, 0, ti, 0)),
        pl.BlockSpec((None, n_heads, head_dim + PV_SUM_ROWS, tm), lambda bi, ti: (bi, 0, 0, ti)),
        pl.BlockSpec((None, npg, a, page), lambda bi, ti: (bi, ti, 0, 0)),
        pl.BlockSpec((None, npg, a, page), lambda bi, ti: (bi, ti, 0, 0)),
        pl.BlockSpec((None, npg, n_heads, page), lambda bi, ti: (bi, ti, 0, 0)),
        pl.BlockSpec((None, tm, pw), lambda bi, ti: (bi, ti, 0)),
    )
    weights = (wqT, wk, wkT, wvT, wf, wfT, bf, bfT, wu)
    in_specs = [
        pl.BlockSpec((None, tm, d), lambda bi, ti: (bi, ti, 0)),
        _resident((1, d)),
        mod_spec(k0),
        mod_spec(k0 + 1),
    ] + [_resident(w.shape) for w in weights]
    return pl.pallas_call(
        functools.partial(_inproj_prompt_kernel, q_scale=q_scale, head_dim=head_dim),
        grid=(b, s // tm),
        in_specs=in_specs,
        out_specs=out_specs,
        out_shape=out_shape,
        scratch_shapes=[pltpu.VMEM((1, LANES), F32), pltpu.VMEM((n_heads, LANES), F32)],
        compiler_params=_params(),
        name="inproj_prompt",
    )(x, g, mods, mods, *weights)


def _attn_prompt_kernel(qTa_ref, ka_ref, vTa_ref, o_ref, m_ref, acc_ref, *, head_dim):
    qi = pl.program_id(2)
    tq = qTa_ref.shape[-1]
    tk = tq
    m_ref[...] = jnp.full_like(m_ref, -jnp.inf)
    acc_ref[...] = jnp.zeros_like(acc_ref)

    def step(j, masked):
        off = pl.multiple_of(j * tk, tk)
        scores = [_dot(ka_ref[i, pl.ds(off, tk), :], qTa_ref[i]) for i in range(2)]
        for i in range(2):
            s = scores[i]
            if masked:
                key = lax.broadcasted_iota(jnp.int32, (tk, tq), 0)
                qry = lax.broadcasted_iota(jnp.int32, (tk, tq), 1)
                s = jnp.where(key <= qry, s, -jnp.inf)
            m_prev = m_ref[i]
            m_new = jnp.maximum(m_prev, jnp.max(s, axis=0, keepdims=True))
            alpha = jnp.exp2(m_prev - m_new)
            p = jnp.exp2(s - m_new)
            acc_ref[i] = alpha * acc_ref[i] + _dot(vTa_ref[i, :, pl.ds(off, tk)], p.astype(BF16))
            m_ref[i] = m_new

    def full_block(j, carry):
        step(j, False)
        return carry

    lax.fori_loop(0, qi, full_block, 0)
    step(qi, True)
    oT = jnp.concatenate(
        [acc_ref[i, :head_dim, :] / acc_ref[i, head_dim:head_dim + 1, :] for i in range(2)], axis=0)
    o_ref[...] = oT.T.astype(BF16)


def _attn_prompt(qTa, ka, vTa, head_dim):
    b, n_heads, depth, s = qTa.shape
    v_rows = vTa.shape[2]
    a = n_heads * head_dim
    tq = ATTN_TILE
    return pl.pallas_call(
        functools.partial(_attn_prompt_kernel, head_dim=head_dim),
        grid=(b, n_heads // 2, s // tq),
        in_specs=[
            pl.BlockSpec((None, 2, depth, tq), lambda bi, hp, qi: (bi, hp, 0, qi)),
            pl.BlockSpec((None, 2, s, depth), lambda bi, hp, qi: (bi, hp, 0, 0)),
            pl.BlockSpec((None, 2, v_rows, s), lambda bi, hp, qi: (bi, hp, 0, 0)),
        ],
        out_specs=pl.BlockSpec((None, tq, depth), lambda bi, hp, qi: (bi, qi, hp)),
        out_shape=jax.ShapeDtypeStruct((b, s, a), BF16),
        scratch_shapes=[
            pltpu.VMEM((2, 1, tq), F32),
            pltpu.VMEM((2, v_rows, tq), F32),
        ],
        compiler_params=_params(),
        name="attn_prompt",
    )(qTa, ka, vTa)


def _pool_prompt_kernel(u_ref, wp_ref, ps_ref, o_ref, halo_ref):
    ti = pl.program_id(1)

    @pl.when(ti == 0)
    def _():
        halo_ref[...] = jnp.zeros_like(halo_ref)

    tm = u_ref.shape[0]
    nh = halo_ref.shape[0]
    gd = wp_ref.shape[1]
    u = u_ref[...]
    ext = jnp.concatenate([halo_ref[...], u], axis=0)
    halo_ref[...] = u[tm - nh:, :]
    sums = {1: ext}
    w = 1
    while w < max(POOL_WINDOWS):
        sums[2 * w] = sums[w] + pltpu.roll(sums[w], w, 0)
        w *= 2
    pos = ti * tm + lax.broadcasted_iota(jnp.int32, (tm, 1), 0)
    ys = []
    for gi, w in enumerate(POOL_WINDOWS):
        lanes = slice(gi * gd, (gi + 1) * gd)
        count = jnp.minimum(pos + 1, w).astype(F32)
        d = sums[w][nh:, lanes] / count - u[:, lanes]
        ys.append(_dot(d.astype(BF16), wp_ref[gi]))
    o_ref[...] = (jnp.concatenate(ys, axis=1) * ps_ref[...]).astype(BF16)


def _pool_prompt(u, wp, ps):
    b, s, pw = u.shape
    tm = ROW_TILE
    halo = 16
    assert halo >= max(POOL_WINDOWS)
    return pl.pallas_call(
        _pool_prompt_kernel,
        grid=(b, s // tm),
        in_specs=[
            pl.BlockSpec((None, tm, pw), lambda bi, ti: (bi, ti, 0)),
            _resident(wp.shape),
            _resident(ps.shape),
        ],
        out_specs=pl.BlockSpec((None, tm, pw), lambda bi, ti: (bi, ti, 0)),
        out_shape=jax.ShapeDtypeStruct((b, s, pw), BF16),
        scratch_shapes=[pltpu.VMEM((halo, pw), F32)],
        compiler_params=_params(),
        name="pool_prompt",
    )(u, wp, ps)


def _outproj_kernel(x_ref, a_ref, p_ref, gt_ref, wa_ref, wp_ref, o_ref):
    x = x_ref[...]
    mix = _dot(a_ref[...], wa_ref[...]) + _dot(p_ref[...], wp_ref[...])
    o_ref[...] = x + _rows(gt_ref[...], x.shape[0]) * mix


def _outproj(x, attn, pool, mods, mod_spec, k, wa, wp):
    r, d = x.shape
    tm = ROW_TILE
    return pl.pallas_call(
        _outproj_kernel,
        grid=(r // tm,),
        in_specs=[
            pl.BlockSpec((tm, d), lambda i: (i, 0)),
            pl.BlockSpec((tm, attn.shape[1]), lambda i: (i, 0)),
            pl.BlockSpec((tm, pool.shape[1]), lambda i: (i, 0)),
            mod_spec(k),
            _resident(wa.shape),
            _resident(wp.shape),
        ],
        out_specs=pl.BlockSpec((tm, d), lambda i: (i, 0)),
        out_shape=jax.ShapeDtypeStruct((r, d), F32),
        compiler_params=_params(),
        name="outproj",
    )(x, attn, pool, mods, wa, wp)


def _inproj_sample_kernel(x_ref, g_ref, sh_ref, sc_ref, wqkv_ref, wkT_ref, wvT_ref, wf_ref, wfT_ref,
                          bf_ref, bfT_ref, wu_ref,
                          q_ref, k_ref, v_ref, kT_ref, vT_ref, lfT_ref, c_ref, u_ref, *, q_scale):
    n_t, _, nb = kT_ref.shape
    n_heads = c_ref.shape[-1]
    h = _modnorm(x_ref[...], g_ref[...], sh_ref[...], sc_ref[...]).astype(BF16)
    z = _dot(h, wqkv_ref[...])
    a = z.shape[1] // 3
    q_ref[...] = (z[:, :a] * q_scale).astype(BF16)
    k_ref[...] = z[:, a:2 * a]
    v_ref[...] = z[:, 2 * a:]
    u_ref[...] = _dot(h, wu_ref[...])
    kT = _dot_nt(wkT_ref[...], h)
    vT = _dot_nt(wvT_ref[...], h)
    lf = _log_sigmoid(_dot(h, wf_ref[...]) + bf_ref[...])
    lfT = _log_sigmoid(_dot_nt(wfT_ref[...], h) + bfT_ref[...])
    run = None
    parts = []
    for t in range(n_t):
        cols = slice(t * nb, (t + 1) * nb)
        kT_ref[t] = kT[:, cols]
        vT_ref[t] = vT[:, cols]
        lfT_ref[t] = lfT[:, cols]
        blk = lf[t * nb:(t + 1) * nb, :]
        run = blk if run is None else run + blk
        parts.append(run)
    c_ref[...] = jnp.concatenate(parts, axis=0)[:, :n_heads]


def _inproj_sample(x, g, mods, mod_spec, k0, wqkv, wkT, wvT, wf, wfT, bf, bfT, wu, n_t, n_heads, q_scale):
    r, d = x.shape
    a = wkT.shape[0]
    pw = wu.shape[1]
    nb = r // n_t
    out_shape = (
        jax.ShapeDtypeStruct((r, a), BF16),
        jax.ShapeDtypeStruct((r, a), F32),
        jax.ShapeDtypeStruct((r, a), F32),
        jax.ShapeDtypeStruct((n_t, a, nb), F32),
        jax.ShapeDtypeStruct((n_t, a, nb), F32),
        jax.ShapeDtypeStruct((n_t, n_heads, nb), F32),
        jax.ShapeDtypeStruct((r, n_heads), F32),
        jax.ShapeDtypeStruct((r, pw), F32),
    )
    whole = lambda shp: pl.BlockSpec(shp, lambda i: (0,) * len(shp))
    in_specs = [whole((r, d)), whole((1, d)), mod_spec(k0), mod_spec(k0 + 1)] + [
        whole(w.shape) for w in (wqkv, wkT, wvT, wf, wfT, bf, bfT, wu)
    ]
    return pl.pallas_call(
        functools.partial(_inproj_sample_kernel, q_scale=q_scale),
        grid=(1,),
        in_specs=in_specs,
        out_specs=tuple(whole(o.shape) for o in out_shape),
        out_shape=out_shape,
        compiler_params=_params(),
        name="inproj_sample",
    )(x, g, mods, mods, wqkv, wkT, wvT, wf, wfT, bf, bfT, wu)


def _decode_kernel(pt_ref, q_ref, cq_ref, c2_ref, kn_ref, vn_ref, kc_ref, vc_ref, lfc_ref,
                   o_ref, kbuf, vbuf, lfbuf, sems, m_ref, l_ref, acc_ref, r_ref, *, head_dim):
    g_n = kbuf.shape[1]
    seq = pl.program_id(0)
    step = pl.program_id(1)
    n_steps = pl.num_programs(1)
    n_pages = n_steps * g_n
    lin = seq * n_steps + step
    slot = lin % 2

    def page_copies(seq_i, step_i, slot_i):
        copies = []
        for i in range(g_n):
            pid = pt_ref[seq_i, n_pages - 1 - (step_i * g_n + i)]
            copies.append(pltpu.make_async_copy(kc_ref.at[pid], kbuf.at[slot_i, i], sems.at[slot_i, 0, i]))
            copies.append(pltpu.make_async_copy(vc_ref.at[pid], vbuf.at[slot_i, i], sems.at[slot_i, 1, i]))
            copies.append(pltpu.make_async_copy(lfc_ref.at[pid], lfbuf.at[slot_i, i], sems.at[slot_i, 2, i]))
        return copies

    @pl.when(lin == 0)
    def _():
        for cp in page_copies(0, 0, 0):
            cp.start()

    @pl.when(lin + 1 < pl.num_programs(0) * n_steps)
    def _():
        wrap = step + 1 == n_steps
        for cp in page_copies(jnp.where(wrap, seq + 1, seq), jnp.where(wrap, 0, step + 1), 1 - slot):
            cp.start()

    for cp in page_copies(seq, step, slot):
        cp.wait()
    k_refs = [kbuf.at[slot, i] for i in range(g_n)]
    v_refs = [vbuf.at[slot, i] for i in range(g_n)]
    lf_refs = [lfbuf.at[slot, i] for i in range(g_n)]
    n_rows, width = q_ref.shape
    n_heads = width // head_dim
    n_t = n_rows // n_heads
    row = lax.broadcasted_iota(jnp.int32, (n_rows, 1), 0)
    lane = lax.broadcasted_iota(jnp.int32, (1, width), 1)
    own = (lane // head_dim) == (row % n_heads)
    t_row = row // n_heads
    q = q_ref[...]
    qbd = jnp.where(own, q, jnp.zeros_like(q))
    cq = cq_ref[...]

    @pl.when(step == 0)
    def _():
        qf = qbd.astype(F32)
        kn = kn_ref[...]
        vn = vn_ref[...]
        c2 = c2_ref[...]
        sc = []
        for s in range(n_t):
            col = jnp.sum(qf * kn[s:s + 1, :], axis=1, keepdims=True) + cq - c2[:, s:s + 1]
            sc.append(jnp.where(t_row >= s, col, -jnp.inf))
        m0 = functools.reduce(jnp.maximum, sc)
        l0 = jnp.zeros_like(m0)
        a0 = jnp.zeros(acc_ref.shape, F32)
        for s in range(n_t):
            p = jnp.exp(sc[s] - m0)
            l0 = l0 + p
            a0 = a0 + p * vn[s:s + 1, :]
        m_ref[...] = m0
        l_ref[...] = l0
        acc_ref[...] = a0
        r_ref[...] = jnp.zeros_like(r_ref)

    page = lf_refs[0].shape[-1]
    pj = lax.broadcasted_iota(jnp.int32, (page, page), 0)
    ps = lax.broadcasted_iota(jnp.int32, (page, page), 1)
    later = (pj > ps).astype(BF16)
    r_run = r_ref[...]
    lf_all = jnp.concatenate([lf_refs[i][...] for i in range(g_n)], axis=0)
    suffix_all = _dot_exact_left(lf_all, later)
    totals = jnp.sum(lf_all, axis=1, keepdims=True)
    scores = []
    for i in range(g_n):
        rows = slice(i * n_heads, (i + 1) * n_heads)
        suffix = suffix_all[rows] + r_run
        r_run = r_run + totals[rows]
        bias = jnp.concatenate([suffix] * n_t, axis=0) + cq
        scores.append(_dot(qbd, k_refs[i][...].astype(BF16)) + bias)
    r_ref[...] = r_run
    m_prev = m_ref[...]
    m_new = functools.reduce(jnp.maximum, [m_prev] + [jnp.max(s, axis=1, keepdims=True) for s in scores])
    alpha = jnp.exp(m_prev - m_new)
    l_new = alpha * l_ref[...]
    acc = alpha * acc_ref[...]
    for i in range(g_n):
        p = jnp.exp(scores[i] - m_new)
        l_new = l_new + jnp.sum(p, axis=1, keepdims=True)
        acc = acc + _dot_nt(p.astype(BF16), v_refs[i][...].astype(BF16))
    m_ref[...] = m_new
    l_ref[...] = l_new
    acc_ref[...] = acc

    @pl.when(step == pl.num_programs(1) - 1)
    def _():
        o = jnp.where(own, acc, 0.0) / l_new
        o_ref[...] = o.reshape(n_t, n_heads, width).sum(axis=1)


def _decode(page_table, q_rep, cq, c2, k_new, v_new, cache_kT, cache_vT, cache_lfT, head_dim):
    bd, n_pages = page_table.shape
    _, n_rows, width = q_rep.shape
    n_t = k_new.shape[1]
    page = cache_kT.shape[-1]
    n_heads = cache_lfT.shape[1]
    g_n = DECODE_PAGES_PER_STEP
    per_seq = lambda shp: pl.BlockSpec((None,) + shp, lambda b, g, pt: (b, 0, 0))
    in_hbm = pl.BlockSpec(memory_space=pl.ANY)
    in_specs = [
        per_seq((n_rows, width)),
        per_seq((n_rows, 1)),
        per_seq((n_rows, n_t)),
        per_seq((n_t, width)),
        per_seq((n_t, width)),
        in_hbm,
        in_hbm,
        in_hbm,
    ]
    grid_spec = pltpu.PrefetchScalarGridSpec(
        num_scalar_prefetch=1,
        grid=(bd, n_pages // g_n),
        in_specs=in_specs,
        out_specs=per_seq((n_t, width)),
        scratch_shapes=[
            pltpu.VMEM((2, g_n, width, page), F32),
            pltpu.VMEM((2, g_n, width, page), F32),
            pltpu.VMEM((2, g_n, n_heads, page), F32),
            pltpu.SemaphoreType.DMA((2, 3, g_n)),
            pltpu.VMEM((n_rows, 1), F32),
            pltpu.VMEM((n_rows, 1), F32),
            pltpu.VMEM((n_rows, width), F32),
            pltpu.VMEM((n_heads, LANES), F32),
        ],
    )
    return pl.pallas_call(
        functools.partial(_decode_kernel, head_dim=head_dim),
        grid_spec=grid_spec,
        out_shape=jax.ShapeDtypeStruct((bd, n_t, width), F32),
        compiler_params=_params(),
        name="decode_attn",
    )(page_table, q_rep, cq, c2, k_new, v_new, cache_kT, cache_vT, cache_lfT)


def _pool_sample_kernel(st_ref, u_ref, wp_ref, ps_ref, o_ref):
    n_state = st_ref.shape[0]
    n_t = u_ref.shape[0]
    gd = wp_ref.shape[1]

    def rows(idx, lanes):
        return st_ref[idx, :, lanes] if idx < n_state else u_ref[idx - n_state, :, lanes]

    for t in range(n_t):
        ys = []
        for gi, w in enumerate(POOL_WINDOWS):
            lanes = slice(gi * gd, (gi + 1) * gd)
            cur = rows(n_state + t, lanes)
            total = cur
            for j in range(1, w):
                total = total + rows(n_state + t - j, lanes)
            d = total / float(w) - cur
            ys.append(_dot(d.astype(BF16), wp_ref[gi]))
        o_ref[t] = (jnp.concatenate(ys, axis=1) * ps_ref[...]).astype(BF16)


def _pool_sample(state_tm, u_tm, wp, ps):
    assert state_tm.shape[0] >= max(POOL_WINDOWS) - 1
    whole = lambda shp: pl.BlockSpec(shp, lambda i: (0,) * len(shp))
    return pl.pallas_call(
        _pool_sample_kernel,
        grid=(1,),
        in_specs=[whole(state_tm.shape), whole(u_tm.shape), whole(wp.shape), whole(ps.shape)],
        out_specs=whole(u_tm.shape),
        out_shape=jax.ShapeDtypeStruct(u_tm.shape, BF16),
        compiler_params=_params(),
        name="pool_sample",
    )(state_tm, u_tm, wp, ps)


def kernel(x_prompt, x_sample, cache_k, cache_v, cache_logf, state_pool, page_table, c_prompt, c_sample, norm_ffn1, norm_mix, norm_ffn2, norm_final, w_ada, b_ada, ffn1_w1, ffn1_w3, ffn1_w2, ffn2_w1, ffn2_w3, ffn2_w2, w_in, b_f, w_o, w_pool, pool_scale):
    bp, s_len, d = x_prompt.shape
    bd, t_len, _ = x_sample.shape
    depth, n_phys, page, n_heads, head_dim = cache_k.shape
    assert depth == 1, "single-layer model"
    a = n_heads * head_dim
    pw = state_pool.shape[-1]
    n_state = state_pool.shape[2]
    q_scale = head_dim ** -0.5
    li = 0

    n_pad = (-(bd + bp)) % 16
    c_all = jnp.concatenate([c_sample, c_prompt, jnp.zeros((n_pad, d), F32)], axis=0)
    mods = _ada(c_all, w_ada[li], b_ada[li][None, :])
    mods_p = mods[bd:bd + bp].reshape(bp, 1, N_MODS * d)
    tiles_per_seq = s_len // ROW_TILE

    def prompt_mod(k):
        return pl.BlockSpec((None, 1, d), lambda i: (i // tiles_per_seq, 0, k))

    def sample_mod(k):
        return pl.BlockSpec((bd, d), lambda i: (0, k))

    f1 = [w[li].astype(BF16) for w in (ffn1_w1, ffn1_w3, ffn1_w2)]
    f2 = [w[li].astype(BF16) for w in (ffn2_w1, ffn2_w3, ffn2_w2)]
    w_in_l = w_in[li]
    wqkv = w_in_l[:, :3 * a].astype(BF16)
    wqT = w_in_l[:, :a].T.astype(BF16)
    wk = w_in_l[:, a:2 * a].astype(BF16)
    wkT = w_in_l[:, a:2 * a].T.astype(BF16)
    wvT = w_in_l[:, 2 * a:3 * a].T.astype(BF16)
    wf_cols = w_in_l[:, 3 * a:3 * a + n_heads]
    wf = jnp.pad(wf_cols, ((0, 0), (0, LANES - n_heads))).astype(BF16)
    wfT = wf_cols.T.astype(BF16)
    bf = jnp.pad(b_f[li][None, :], ((0, 0), (0, LANES - n_heads)))
    bfT = b_f[li][:, None]
    wu = w_in_l[:, 3 * a + n_heads:].astype(BF16)
    wo_a = w_o[li][:a].astype(BF16)
    wo_p = w_o[li][a:].astype(BF16)
    wp = w_pool[li].astype(BF16)
    ps = pool_scale[li][None, :]
    g1 = norm_ffn1[li][None, :]
    gm = norm_mix[li][None, :]
    g2 = norm_ffn2[li][None, :]
    gf = norm_final[None, :]

    xp = x_prompt.reshape(bp * s_len, d)
    xp = _ffn(xp, g1, mods_p, prompt_mod, 0, *f1)
    qTa_p, ka_p, vTb_p, kT_p, vT_p, lfT_p, u_p = _inproj_prompt(
        xp.reshape(bp, s_len, d), gm, mods_p, 3, wqT, wk, wkT, wvT, wf, wfT, bf, bfT, wu,
        page, n_heads, head_dim, q_scale)
    attn_p = _attn_prompt(qTa_p, ka_p, vTb_p, head_dim)
    pool_p = _pool_prompt(u_p, wp, ps)
    xp = _outproj(xp, attn_p.reshape(bp * s_len, a), pool_p.reshape(bp * s_len, pw), mods_p, prompt_mod, 5, wo_a, wo_p)
    y_prompt = _ffn(xp, g2, mods_p, prompt_mod, 6, *f2, gf=gf).reshape(bp, s_len, d)

    n_pp = s_len // page
    new_k_prompt = kT_p.reshape(bp, n_pp, n_heads, head_dim, page).transpose(0, 1, 4, 2, 3)[None]
    new_v_prompt = vT_p.reshape(bp, n_pp, n_heads, head_dim, page).transpose(0, 1, 4, 2, 3)[None]
    new_logf_prompt = lfT_p.transpose(0, 1, 3, 2)[None]
    new_pool_prompt = u_p[:, s_len - n_state:][None]

    n_rows = bd * t_len
    xs = x_sample.transpose(1, 0, 2).reshape(n_rows, d)
    xs = _ffn(xs, g1, mods, sample_mod, 0, *f1)
    q_s, k_s, v_s, kT_s, vT_s, lfT_s, c_s, u_s = _inproj_sample(
        xs, gm, mods, sample_mod, 3, wqkv, wkT, wvT, wf, wfT, bf, bfT, wu, t_len, n_heads, q_scale)

    def per_seq(z):
        return z.reshape(t_len, bd, z.shape[-1]).transpose(1, 0, 2)

    q_rep = jnp.repeat(per_seq(q_s), n_heads, axis=1)
    c_bt = per_seq(c_s)
    cq = c_bt.reshape(bd, t_len * n_heads, 1)
    c2 = jnp.broadcast_to(c_bt.transpose(0, 2, 1)[:, None], (bd, t_len, n_heads, t_len)).reshape(bd, t_len * n_heads, t_len)
    cache_kT = cache_k[li].transpose(0, 2, 3, 1).reshape(n_phys, a, page)
    cache_vT = cache_v[li].transpose(0, 2, 3, 1).reshape(n_phys, a, page)
    cache_lfT = cache_logf[li].transpose(0, 2, 1)
    attn_s = _decode(page_table, q_rep, cq, c2, per_seq(k_s), per_seq(v_s), cache_kT, cache_vT, cache_lfT, head_dim)
    attn_s = attn_s.transpose(1, 0, 2).reshape(n_rows, a).astype(BF16)
    state_tm = state_pool[li].transpose(1, 0, 2)
    u_tm = u_s.reshape(t_len, bd, pw)
    pool_s = _pool_sample(state_tm, u_tm, wp, ps).reshape(n_rows, pw)
    xs = _outproj(xs, attn_s, pool_s, mods, sample_mod, 5, wo_a, wo_p)
    y_sample = _ffn(xs, g2, mods, sample_mod, 6, *f2, gf=gf).reshape(t_len, bd, d).transpose(1, 0, 2)

    new_k_sample = kT_s.reshape(t_len, n_heads, head_dim, bd).transpose(3, 0, 1, 2)[None]
    new_v_sample = vT_s.reshape(t_len, n_heads, head_dim, bd).transpose(3, 0, 1, 2)[None]
    new_logf_sample = lfT_s.transpose(2, 0, 1)[None]
    new_pool_sample = jnp.concatenate([state_tm[t_len:], u_tm], axis=0).transpose(1, 0, 2)[None]

    return (y_prompt, y_sample, new_k_prompt, new_v_prompt, new_logf_prompt, new_pool_prompt,
            new_k_sample, new_v_sample, new_logf_sample, new_pool_sample)
```

```python
import functools

import jax
import jax.numpy as jnp
from jax import lax
from jax.experimental import pallas as pl
from jax.experimental.pallas import tpu as pltpu

F32 = jnp.float32
BF16 = jnp.bfloat16

RMS_EPS = 1e-6
POOL_WINDOWS = (2, 4, 8, 16)
N_MODS = 9
LANES = 128
VMEM_LIMIT_BYTES = 56 * 1024 * 1024

ROW_TILE = 512
FFN_CHUNK = 256
ATTN_TILE = 512
DECODE_PAGES_PER_STEP = 16

_NT_DIMS = (((1,), (1,)), ((), ()))


def _dot(a, b):
    return jnp.dot(a, b, preferred_element_type=F32)


def _dot_nt(a, b):
    return lax.dot_general(a, b, _NT_DIMS, preferred_element_type=F32)


def _split3(x):
    hi = x.astype(BF16)
    r = x - hi.astype(F32)
    mid = r.astype(BF16)
    lo = (r - mid.astype(F32)).astype(BF16)
    return hi, mid, lo


def _split3_f32(x):
    hi, mid, lo = _split3(x)
    return hi.astype(F32), mid.astype(F32), lo.astype(F32)


def _dot_exact_left(x, ones_mat):
    hi, mid, lo = _split3(x)
    return _dot(hi, ones_mat) + _dot(mid, ones_mat) + _dot(lo, ones_mat)


def _dot_exact_right(ones_mat, x):
    hi, mid, lo = _split3(x)
    return _dot(ones_mat, hi) + _dot(ones_mat, mid) + _dot(ones_mat, lo)


def _silu(x):
    return x * jax.nn.sigmoid(x)


def _log_sigmoid(x):
    return jnp.minimum(x, 0.0) - jnp.log1p(jnp.exp(-jnp.abs(x)))


def _rmsnorm(x, g):
    return x * lax.rsqrt(jnp.mean(x * x, axis=-1, keepdims=True) + RMS_EPS) * g


def _rows(m, n):
    if m.shape[0] in (1, n):
        return m
    return jnp.concatenate([m] * (n // m.shape[0]), axis=0)


def _modnorm(x, g, shift, scale):
    n = x.shape[0]
    return _rmsnorm(x, g) * (1.0 + _rows(scale, n)) + _rows(shift, n)


def _resident(shape):
    zeros = (0,) * len(shape)
    return pl.BlockSpec(shape, lambda *_: zeros, pipeline_mode=pl.Buffered(1))


def _params():
    return pltpu.CompilerParams(vmem_limit_bytes=VMEM_LIMIT_BYTES)


def _ada_kernel(c_ref, w_ref, b_ref, o_ref):
    s = _silu(c_ref[...]).astype(BF16)
    o_ref[...] = _dot(s, w_ref[...].astype(BF16)) + b_ref[...]


def _ada(c, w, b):
    n, d = c.shape
    m = w.shape[1]
    tn = d
    return pl.pallas_call(
        _ada_kernel,
        grid=(m // tn,),
        in_specs=[
            pl.BlockSpec((n, d), lambda j: (0, 0)),
            pl.BlockSpec((d, tn), lambda j: (0, j)),
            pl.BlockSpec((1, tn), lambda j: (0, j)),
        ],
        out_specs=pl.BlockSpec((n, tn), lambda j: (0, j)),
        out_shape=jax.ShapeDtypeStruct((n, m), F32),
        compiler_params=_params(),
        name="ada_params",
    )(c, w, b)


def _ffn_kernel(x_ref, g_ref, sh_ref, sc_ref, gt_ref, w1_ref, w3_ref, w2_ref, *rest, final_norm):
    if final_norm:
        gf_ref, o_ref, h_ref, acc_ref = rest
    else:
        o_ref, h_ref, acc_ref = rest
    x = x_ref[...]
    n = x.shape[0]
    h_ref[...] = _modnorm(x, g_ref[...], sh_ref[...], sc_ref[...]).astype(BF16)
    acc_ref[...] = jnp.zeros_like(acc_ref)

    def chunk(c, carry):
        off = pl.multiple_of(c * FFN_CHUNK, FFN_CHUNK)
        h = h_ref[...]
        a = _dot(h, w1_ref[:, pl.ds(off, FFN_CHUNK)])
        b = _dot(h, w3_ref[:, pl.ds(off, FFN_CHUNK)])
        act = (_silu(a) * b).astype(BF16)
        acc_ref[...] += _dot(act, w2_ref[pl.ds(off, FFN_CHUNK), :])
        return carry

    lax.fori_loop(0, w1_ref.shape[1] // FFN_CHUNK, chunk, 0, unroll=True)
    out = x + 0.5 * _rows(gt_ref[...], n) * acc_ref[...]
    if final_norm:
        out = _rmsnorm(out, gf_ref[...])
    o_ref[...] = out


def _ffn(x, g, mods, mod_spec, k0, w1, w3, w2, gf=None):
    r, d = x.shape
    dff = w1.shape[1]
    tm = ROW_TILE
    in_specs = [
        pl.BlockSpec((tm, d), lambda i: (i, 0)),
        _resident((1, d)),
        mod_spec(k0),
        mod_spec(k0 + 1),
        mod_spec(k0 + 2),
        _resident((d, dff)),
        _resident((d, dff)),
        _resident((dff, d)),
    ]
    args = [x, g, mods, mods, mods, w1, w3, w2]
    if gf is not None:
        in_specs.append(_resident((1, d)))
        args.append(gf)
    return pl.pallas_call(
        functools.partial(_ffn_kernel, final_norm=gf is not None),
        grid=(r // tm,),
        in_specs=in_specs,
        out_specs=pl.BlockSpec((tm, d), lambda i: (i, 0)),
        out_shape=jax.ShapeDtypeStruct((r, d), F32),
        scratch_shapes=[pltpu.VMEM((tm, d), BF16), pltpu.VMEM((tm, d), F32)],
        compiler_params=_params(),
        name="ffn_final" if gf is not None else "ffn",
    )(*args)


N_BIAS_PARTS = 3
LOG2_E = 1.4426950408889634
PV_SUM_ROWS = 16


def _inproj_prompt_kernel(x_ref, g_ref, sh_ref, sc_ref, wqT_ref, wkT_ref, wvT_ref, wf_ref, wfT_ref,
                          bf_ref, bfT_ref, wu_ref,
                          qTa_ref, ka_ref, vTa_ref, kT_ref, vT_ref, lfT_ref, u_ref,
                          carry_ref, carryT_ref, *, q_scale, head_dim):
    @pl.when(pl.program_id(1) == 0)
    def _():
        carry_ref[...] = jnp.zeros_like(carry_ref)
        carryT_ref[...] = jnp.zeros_like(carryT_ref)

    tm = x_ref.shape[0]
    page = kT_ref.shape[-1]
    n_heads, depth, _ = qTa_ref.shape
    h = _modnorm(x_ref[...], g_ref[...], sh_ref[...], sc_ref[...]).astype(BF16)
    u_ref[...] = _dot(h, wu_ref[...])
    kT = _dot_nt(wkT_ref[...], h)
    vT = _dot_nt(wvT_ref[...], h)
    lf = _log_sigmoid(_dot(h, wf_ref[...]) + bf_ref[...])
    lfT = _log_sigmoid(_dot_nt(wfT_ref[...], h) + bfT_ref[...])
    for j in range(tm // page):
        cols = slice(j * page, (j + 1) * page)
        kT_ref[j] = kT[:, cols]
        vT_ref[j] = vT[:, cols]
        lfT_ref[j] = lfT[:, cols]
    r = lax.broadcasted_iota(jnp.int32, (tm, tm), 0)
    c = lax.broadcasted_iota(jnp.int32, (tm, tm), 1)
    cf = _dot_exact_right((c <= r).astype(BF16), lf) + carry_ref[...]
    carry_ref[...] = cf[tm - 1:tm, :]
    cfT = _dot_exact_left(lfT, (r <= c).astype(BF16)) + carryT_ref[:, :1]
    carryT_ref[...] = jnp.broadcast_to(cfT[:, tm - 1:tm], carryT_ref.shape)
    qT = _dot_nt(wqT_ref[...], h) * (q_scale * LOG2_E)
    k = kT.T
    sub = lax.broadcasted_iota(jnp.int32, (8, tm), 0)
    lane = lax.broadcasted_iota(jnp.int32, (tm, depth), 1)
    pad_rows = jnp.zeros((depth - head_dim - 8, tm), F32)
    sum_rows = vTa_ref.shape[1] - head_dim
    ones_row = jnp.where(lax.broadcasted_iota(jnp.int32, (sum_rows, tm), 0) == 0, 1.0, 0.0)
    for hd in range(n_heads):
        rows = slice(hd * head_dim, (hd + 1) * head_dim)
        vTa_ref[hd] = jnp.concatenate([vT[rows, :], ones_row], axis=0).astype(BF16)
        f_hi, f_mid, f_lo = _split3_f32(jnp.broadcast_to(cfT[hd:hd + 1, :] * LOG2_E, (8, tm)))
        q_bias = jnp.where(sub == 0, f_hi, jnp.where(sub == 1, f_mid, jnp.where(sub == 2, f_lo,
                 jnp.where(sub < 2 * N_BIAS_PARTS, 1.0, 0.0))))
        qTa_ref[hd] = jnp.concatenate([qT[rows, :], q_bias, pad_rows], axis=0).astype(BF16)
        n_hi, n_mid, n_lo = _split3_f32(jnp.broadcast_to(cf[:, hd:hd + 1] * -LOG2_E, (tm, depth)))
        k_bias = jnp.where(lane == head_dim + 3, n_hi, jnp.where(lane == head_dim + 4, n_mid,
                 jnp.where(lane == head_dim + 5, n_lo, jnp.where(lane < head_dim + N_BIAS_PARTS, 1.0, 0.0))))
        blk = k[:, (hd // 2) * depth:(hd // 2 + 1) * depth]
        if hd % 2:
            blk = pltpu.roll(blk, head_dim, 1)
        ka_ref[hd] = jnp.where(lane < head_dim, blk, k_bias).astype(BF16)


def _inproj_prompt(x, g, mods, k0, wqT, wkT, wvT, wf, wfT, bf, bfT, wu, page, n_heads, head_dim, q_scale):
    b, s, d = x.shape
    a = wkT.shape[0]
    pw = wu.shape[1]
    tm = ROW_TILE
    npg = tm // page
    depth = 2 * head_dim
    assert depth == LANES and depth - head_dim >= 8

    def mod_spec(k):
        return pl.BlockSpec((None, 1, d), lambda bi, ti: (bi, 0, k))

    out_shape = (
        jax.ShapeDtypeStruct((b, n_heads, depth, s), BF16),
        jax.ShapeDtypeStruct((b, n_heads, s, depth), BF16),
        jax.ShapeDtypeStruct((b, n_heads, head_dim + PV_SUM_ROWS, s), BF16),
        jax.ShapeDtypeStruct((b, s // page, a, page), F32),
        jax.ShapeDtypeStruct((b, s // page, a, page), F32),
        jax.ShapeDtypeStruct((b, s // page, n_heads, page), F32),
        jax.ShapeDtypeStruct((b, s, pw), F32),
    )
    out_specs = (
        pl.BlockSpec((None, n_heads, depth, tm), lambda bi, ti: (bi, 0, 0, ti)),
        pl.BlockSpec((None, n_heads, tm, depth), lambda bi, ti: (bi, 0, ti, 0)),
        pl.BlockSpec((None, n_heads, head_dim + PV_SUM_ROWS, tm), lambda bi, ti: (bi, 0, 0, ti)),
        pl.BlockSpec((None, npg, a, page), lambda bi, ti: (bi, ti, 0, 0)),
        pl.BlockSpec((None, npg, a, page), lambda bi, ti: (bi, ti, 0, 0)),
        pl.BlockSpec((None, npg, n_heads, page), lambda bi, ti: (bi, ti, 0, 0)),
        pl.BlockSpec((None, tm, pw), lambda bi, ti: (bi, ti, 0)),
    )
    weights = (wqT, wkT, wvT, wf, wfT, bf, bfT, wu)
    in_specs = [
        pl.BlockSpec((None, tm, d), lambda bi, ti: (bi, ti, 0)),
        _resident((1, d)),
        mod_spec(k0),
        mod_spec(k0 + 1),
    ] + [_resident(w.shape) for w in weights]
    return pl.pallas_call(
        functools.partial(_inproj_prompt_kernel, q_scale=q_scale, head_dim=head_dim),
        grid=(b, s // tm),
        in_specs=in_specs,
        out_specs=out_specs,
        out_shape=out_shape,
        scratch_shapes=[pltpu.VMEM((1, LANES), F32), pltpu.VMEM((n_heads, LANES), F32)],
        compiler_params=_params(),
        name="inproj_prompt",
    )(x, g, mods, mods, *weights)


def _attn_prompt_kernel(qTa_ref, ka_ref, vTa_ref, o_ref, m_ref, acc_ref, s_ref, *, head_dim):
    qi = pl.program_id(2)
    tq = qTa_ref.shape[-1]
    tk = tq
    m_ref[...] = jnp.full_like(m_ref, -jnp.inf)
    acc_ref[...] = jnp.zeros_like(acc_ref)

    def scores_into(slot, j):
        off = pl.multiple_of(j * tk, tk)
        for i in range(2):
            s_ref[slot, i] = _dot(ka_ref[i, pl.ds(off, tk), :], qTa_ref[i])

    def consume(slot, j, masked):
        off = pl.multiple_of(j * tk, tk)
        for i in range(2):
            s = s_ref[slot, i]
            if masked:
                key = lax.broadcasted_iota(jnp.int32, (tk, tq), 0)
                qry = lax.broadcasted_iota(jnp.int32, (tk, tq), 1)
                s = jnp.where(key <= qry, s, -jnp.inf)
            m_prev = m_ref[i]
            m_new = jnp.maximum(m_prev, jnp.max(s, axis=0, keepdims=True))
            alpha = jnp.exp2(m_prev - m_new)
            p = jnp.exp2(s - m_new)
            acc_ref[i] = alpha * acc_ref[i] + _dot(vTa_ref[i, :, pl.ds(off, tk)], p.astype(BF16))
            m_ref[i] = m_new

    def two_full_blocks(jj, carry):
        scores_into(1, 2 * jj + 1)
        consume(0, 2 * jj, False)
        scores_into(0, 2 * jj + 2)
        consume(1, 2 * jj + 1, False)
        return carry

    scores_into(0, 0)
    lax.fori_loop(0, qi // 2, two_full_blocks, 0)

    @pl.when(qi % 2 == 0)
    def _():
        consume(0, qi, True)

    @pl.when(qi % 2 == 1)
    def _():
        scores_into(1, qi)
        consume(0, qi - 1, False)
        consume(1, qi, True)

    oT = jnp.concatenate(
        [acc_ref[i, :head_dim, :] / acc_ref[i, head_dim:head_dim + 1, :] for i in range(2)], axis=0)
    o_ref[...] = oT.T.astype(BF16)


def _attn_prompt(qTa, ka, vTa, head_dim):
    b, n_heads, depth, s = qTa.shape
    v_rows = vTa.shape[2]
    a = n_heads * head_dim
    tq = ATTN_TILE
    return pl.pallas_call(
        functools.partial(_attn_prompt_kernel, head_dim=head_dim),
        grid=(b, n_heads // 2, s // tq),
        in_specs=[
            pl.BlockSpec((None, 2, depth, tq), lambda bi, hp, qi: (bi, hp, 0, qi)),
            pl.BlockSpec((None, 2, s, depth), lambda bi, hp, qi: (bi, hp, 0, 0)),
            pl.BlockSpec((None, 2, v_rows, s), lambda bi, hp, qi: (bi, hp, 0, 0)),
        ],
        out_specs=pl.BlockSpec((None, tq, depth), lambda bi, hp, qi: (bi, qi, hp)),
        out_shape=jax.ShapeDtypeStruct((b, s, a), BF16),
        scratch_shapes=[
            pltpu.VMEM((2, 1, tq), F32),
            pltpu.VMEM((2, v_rows, tq), F32),
            pltpu.VMEM((2, 2, tq, tq), F32),
        ],
        compiler_params=_params(),
        name="attn_prompt",
    )(qTa, ka, vTa)


def _pool_prompt_kernel(u_ref, wp_ref, ps_ref, o_ref, halo_ref):
    ti = pl.program_id(1)

    @pl.when(ti == 0)
    def _():
        halo_ref[...] = jnp.zeros_like(halo_ref)

    tm = u_ref.shape[0]
    nh = halo_ref.shape[0]
    gd = wp_ref.shape[1]
    u = u_ref[...]
    ext = jnp.concatenate([halo_ref[...], u], axis=0)
    halo_ref[...] = u[tm - nh:, :]
    sums = {1: ext}
    w = 1
    while w < max(POOL_WINDOWS):
        sums[2 * w] = sums[w] + pltpu.roll(sums[w], w, 0)
        w *= 2
    pos = ti * tm + lax.broadcasted_iota(jnp.int32, (tm, 1), 0)
    ys = []
    for gi, w in enumerate(POOL_WINDOWS):
        lanes = slice(gi * gd, (gi + 1) * gd)
        count = jnp.minimum(pos + 1, w).astype(F32)
        d = sums[w][nh:, lanes] / count - u[:, lanes]
        ys.append(_dot(d.astype(BF16), wp_ref[gi]))
    o_ref[...] = (jnp.concatenate(ys, axis=1) * ps_ref[...]).astype(BF16)


def _pool_prompt(u, wp, ps):
    b, s, pw = u.shape
    tm = ROW_TILE
    halo = 16
    assert halo >= max(POOL_WINDOWS)
    return pl.pallas_call(
        _pool_prompt_kernel,
        grid=(b, s // tm),
        in_specs=[
            pl.BlockSpec((None, tm, pw), lambda bi, ti: (bi, ti, 0)),
            _resident(wp.shape),
            _resident(ps.shape),
        ],
        out_specs=pl.BlockSpec((None, tm, pw), lambda bi, ti: (bi, ti, 0)),
        out_shape=jax.ShapeDtypeStruct((b, s, pw), BF16),
        scratch_shapes=[pltpu.VMEM((halo, pw), F32)],
        compiler_params=_params(),
        name="pool_prompt",
    )(u, wp, ps)


def _outproj_kernel(x_ref, a_ref, p_ref, gt_ref, wa_ref, wp_ref, o_ref):
    x = x_ref[...]
    mix = _dot(a_ref[...], wa_ref[...]) + _dot(p_ref[...], wp_ref[...])
    o_ref[...] = x + _rows(gt_ref[...], x.shape[0]) * mix


def _outproj(x, attn, pool, mods, mod_spec, k, wa, wp):
    r, d = x.shape
    tm = ROW_TILE
    return pl.pallas_call(
        _outproj_kernel,
        grid=(r // tm,),
        in_specs=[
            pl.BlockSpec((tm, d), lambda i: (i, 0)),
            pl.BlockSpec((tm, attn.shape[1]), lambda i: (i, 0)),
            pl.BlockSpec((tm, pool.shape[1]), lambda i: (i, 0)),
            mod_spec(k),
            _resident(wa.shape),
            _resident(wp.shape),
        ],
        out_specs=pl.BlockSpec((tm, d), lambda i: (i, 0)),
        out_shape=jax.ShapeDtypeStruct((r, d), F32),
        compiler_params=_params(),
        name="outproj",
    )(x, attn, pool, mods, wa, wp)


def _inproj_sample_kernel(x_ref, g_ref, sh_ref, sc_ref, wqkv_ref, wkT_ref, wvT_ref, wf_ref, wfT_ref,
                          bf_ref, bfT_ref, wu_ref,
                          q_ref, k_ref, v_ref, kT_ref, vT_ref, lfT_ref, c_ref, u_ref, *, q_scale):
    n_t, _, nb = kT_ref.shape
    n_heads = c_ref.shape[-1]
    h = _modnorm(x_ref[...], g_ref[...], sh_ref[...], sc_ref[...]).astype(BF16)
    z = _dot(h, wqkv_ref[...])
    a = z.shape[1] // 3
    q_ref[...] = (z[:, :a] * q_scale).astype(BF16)
    k_ref[...] = z[:, a:2 * a]
    v_ref[...] = z[:, 2 * a:]
    u_ref[...] = _dot(h, wu_ref[...])
    kT = _dot_nt(wkT_ref[...], h)
    vT = _dot_nt(wvT_ref[...], h)
    lf = _log_sigmoid(_dot(h, wf_ref[...]) + bf_ref[...])
    lfT = _log_sigmoid(_dot_nt(wfT_ref[...], h) + bfT_ref[...])
    run = None
    parts = []
    for t in range(n_t):
        cols = slice(t * nb, (t + 1) * nb)
        kT_ref[t] = kT[:, cols]
        vT_ref[t] = vT[:, cols]
        lfT_ref[t] = lfT[:, cols]
        blk = lf[t * nb:(t + 1) * nb, :]
        run = blk if run is None else run + blk
        parts.append(run)
    c_ref[...] = jnp.concatenate(parts, axis=0)[:, :n_heads]


def _inproj_sample(x, g, mods, mod_spec, k0, wqkv, wkT, wvT, wf, wfT, bf, bfT, wu, n_t, n_heads, q_scale):
    r, d = x.shape
    a = wkT.shape[0]
    pw = wu.shape[1]
    nb = r // n_t
    out_shape = (
        jax.ShapeDtypeStruct((r, a), BF16),
        jax.ShapeDtypeStruct((r, a), F32),
        jax.ShapeDtypeStruct((r, a), F32),
        jax.ShapeDtypeStruct((n_t, a, nb), F32),
        jax.ShapeDtypeStruct((n_t, a, nb), F32),
        jax.ShapeDtypeStruct((n_t, n_heads, nb), F32),
        jax.ShapeDtypeStruct((r, n_heads), F32),
        jax.ShapeDtypeStruct((r, pw), F32),
    )
    whole = lambda shp: pl.BlockSpec(shp, lambda i: (0,) * len(shp))
    in_specs = [whole((r, d)), whole((1, d)), mod_spec(k0), mod_spec(k0 + 1)] + [
        whole(w.shape) for w in (wqkv, wkT, wvT, wf, wfT, bf, bfT, wu)
    ]
    return pl.pallas_call(
        functools.partial(_inproj_sample_kernel, q_scale=q_scale),
        grid=(1,),
        in_specs=in_specs,
        out_specs=tuple(whole(o.shape) for o in out_shape),
        out_shape=out_shape,
        compiler_params=_params(),
        name="inproj_sample",
    )(x, g, mods, mods, wqkv, wkT, wvT, wf, wfT, bf, bfT, wu)


def _decode_kernel(pt_ref, q_ref, cq_ref, c2_ref, kn_ref, vn_ref, kc_ref, vc_ref, lfc_ref,
                   o_ref, kbuf, vbuf, lfbuf, sems, m_ref, l_ref, acc_ref, r_ref, *, head_dim):
    g_n = kbuf.shape[1]
    seq = pl.program_id(0)
    step = pl.program_id(1)
    n_steps = pl.num_programs(1)
    n_pages = n_steps * g_n
    lin = seq * n_steps + step
    slot = lin % 2

    def page_copies(seq_i, step_i, slot_i):
        copies = []
        for i in range(g_n):
            pid = pt_ref[seq_i, n_pages - 1 - (step_i * g_n + i)]
            copies.append(pltpu.make_async_copy(kc_ref.at[pid], kbuf.at[slot_i, i], sems.at[slot_i, 0, i]))
            copies.append(pltpu.make_async_copy(vc_ref.at[pid], vbuf.at[slot_i, i], sems.at[slot_i, 1, i]))
            copies.append(pltpu.make_async_copy(lfc_ref.at[pid], lfbuf.at[slot_i, i], sems.at[slot_i, 2, i]))
        return copies

    @pl.when(lin == 0)
    def _():
        for cp in page_copies(0, 0, 0):
            cp.start()

    @pl.when(lin + 1 < pl.num_programs(0) * n_steps)
    def _():
        wrap = step + 1 == n_steps
        for cp in page_copies(jnp.where(wrap, seq + 1, seq), jnp.where(wrap, 0, step + 1), 1 - slot):
            cp.start()

    for cp in page_copies(seq, step, slot):
        cp.wait()
    k_refs = [kbuf.at[slot, i] for i in range(g_n)]
    v_refs = [vbuf.at[slot, i] for i in range(g_n)]
    lf_refs = [lfbuf.at[slot, i] for i in range(g_n)]
    n_rows, width = q_ref.shape
    n_heads = width // head_dim
    n_t = n_rows // n_heads
    row = lax.broadcasted_iota(jnp.int32, (n_rows, 1), 0)
    lane = lax.broadcasted_iota(jnp.int32, (1, width), 1)
    own = (lane // head_dim) == (row % n_heads)
    t_row = row // n_heads
    q = q_ref[...]
    qbd = jnp.where(own, q, jnp.zeros_like(q))
    cq = cq_ref[...]

    @pl.when(step == 0)
    def _():
        qf = qbd.astype(F32)
        kn = kn_ref[...]
        vn = vn_ref[...]
        c2 = c2_ref[...]
        sc = []
        for s in range(n_t):
            col = jnp.sum(qf * kn[s:s + 1, :], axis=1, keepdims=True) + cq - c2[:, s:s + 1]
            sc.append(jnp.where(t_row >= s, col, -jnp.inf))
        m0 = functools.reduce(jnp.maximum, sc)
        l0 = jnp.zeros_like(m0)
        a0 = jnp.zeros(acc_ref.shape, F32)
        for s in range(n_t):
            p = jnp.exp(sc[s] - m0)
            l0 = l0 + p
            a0 = a0 + p * vn[s:s + 1, :]
        m_ref[...] = m0
        l_ref[...] = l0
        acc_ref[...] = a0
        r_ref[...] = jnp.zeros_like(r_ref)

    page = lf_refs[0].shape[-1]
    pj = lax.broadcasted_iota(jnp.int32, (page, page), 0)
    ps = lax.broadcasted_iota(jnp.int32, (page, page), 1)
    later = (pj > ps).astype(BF16)
    r_run = r_ref[...]
    lf_all = jnp.concatenate([lf_refs[i][...] for i in range(g_n)], axis=0)
    suffix_all = _dot_exact_left(lf_all, later)
    totals = jnp.sum(lf_all, axis=1, keepdims=True)
    scores = []
    for i in range(g_n):
        rows = slice(i * n_heads, (i + 1) * n_heads)
        suffix = suffix_all[rows] + r_run
        r_run = r_run + totals[rows]
        bias = jnp.concatenate([suffix] * n_t, axis=0) + cq
        scores.append(_dot(qbd, k_refs[i][...].astype(BF16)) + bias)
    r_ref[...] = r_run
    m_prev = m_ref[...]
    m_new = functools.reduce(jnp.maximum, [m_prev] + [jnp.max(s, axis=1, keepdims=True) for s in scores])
    alpha = jnp.exp(m_prev - m_new)
    l_new = alpha * l_ref[...]
    acc = alpha * acc_ref[...]
    for i in range(g_n):
        p = jnp.exp(scores[i] - m_new)
        l_new = l_new + jnp.sum(p, axis=1, keepdims=True)
        acc = acc + _dot_nt(p.astype(BF16), v_refs[i][...].astype(BF16))
    m_ref[...] = m_new
    l_ref[...] = l_new
    acc_ref[...] = acc

    @pl.when(step == pl.num_programs(1) - 1)
    def _():
        o = jnp.where(own, acc, 0.0) / l_new
        o_ref[...] = o.reshape(n_t, n_heads, width).sum(axis=1)


def _decode(page_table, q_rep, cq, c2, k_new, v_new, cache_kT, cache_vT, cache_lfT, head_dim):
    bd, n_pages = page_table.shape
    _, n_rows, width = q_rep.shape
    n_t = k_new.shape[1]
    page = cache_kT.shape[-1]
    n_heads = cache_lfT.shape[1]
    g_n = DECODE_PAGES_PER_STEP
    per_seq = lambda shp: pl.BlockSpec((None,) + shp, lambda b, g, pt: (b, 0, 0))
    in_hbm = pl.BlockSpec(memory_space=pl.ANY)
    in_specs = [
        per_seq((n_rows, width)),
        per_seq((n_rows, 1)),
        per_seq((n_rows, n_t)),
        per_seq((n_t, width)),
        per_seq((n_t, width)),
        in_hbm,
        in_hbm,
        in_hbm,
    ]
    grid_spec = pltpu.PrefetchScalarGridSpec(
        num_scalar_prefetch=1,
        grid=(bd, n_pages // g_n),
        in_specs=in_specs,
        out_specs=per_seq((n_t, width)),
        scratch_shapes=[
            pltpu.VMEM((2, g_n, width, page), F32),
            pltpu.VMEM((2, g_n, width, page), F32),
            pltpu.VMEM((2, g_n, n_heads, page), F32),
            pltpu.SemaphoreType.DMA((2, 3, g_n)),
            pltpu.VMEM((n_rows, 1), F32),
            pltpu.VMEM((n_rows, 1), F32),
            pltpu.VMEM((n_rows, width), F32),
            pltpu.VMEM((n_heads, LANES), F32),
        ],
    )
    return pl.pallas_call(
        functools.partial(_decode_kernel, head_dim=head_dim),
        grid_spec=grid_spec,
        out_shape=jax.ShapeDtypeStruct((bd, n_t, width), F32),
        compiler_params=_params(),
        name="decode_attn",
    )(page_table, q_rep, cq, c2, k_new, v_new, cache_kT, cache_vT, cache_lfT)


def _pool_sample_kernel(st_ref, u_ref, wp_ref, ps_ref, o_ref):
    n_state = st_ref.shape[0]
    n_t = u_ref.shape[0]
    gd = wp_ref.shape[1]

    def rows(idx, lanes):
        return st_ref[idx, :, lanes] if idx < n_state else u_ref[idx - n_state, :, lanes]

    for t in range(n_t):
        ys = []
        for gi, w in enumerate(POOL_WINDOWS):
            lanes = slice(gi * gd, (gi + 1) * gd)
            cur = rows(n_state + t, lanes)
            total = cur
            for j in range(1, w):
                total = total + rows(n_state + t - j, lanes)
            d = total / float(w) - cur
            ys.append(_dot(d.astype(BF16), wp_ref[gi]))
        o_ref[t] = (jnp.concatenate(ys, axis=1) * ps_ref[...]).astype(BF16)


def _pool_sample(state_tm, u_tm, wp, ps):
    assert state_tm.shape[0] >= max(POOL_WINDOWS) - 1
    whole = lambda shp: pl.BlockSpec(shp, lambda i: (0,) * len(shp))
    return pl.pallas_call(
        _pool_sample_kernel,
        grid=(1,),
        in_specs=[whole(state_tm.shape), whole(u_tm.shape), whole(wp.shape), whole(ps.shape)],
        out_specs=whole(u_tm.shape),
        out_shape=jax.ShapeDtypeStruct(u_tm.shape, BF16),
        compiler_params=_params(),
        name="pool_sample",
    )(state_tm, u_tm, wp, ps)


def kernel(x_prompt, x_sample, cache_k, cache_v, cache_logf, state_pool, page_table, c_prompt, c_sample, norm_ffn1, norm_mix, norm_ffn2, norm_final, w_ada, b_ada, ffn1_w1, ffn1_w3, ffn1_w2, ffn2_w1, ffn2_w3, ffn2_w2, w_in, b_f, w_o, w_pool, pool_scale):
    bp, s_len, d = x_prompt.shape
    bd, t_len, _ = x_sample.shape
    depth, n_phys, page, n_heads, head_dim = cache_k.shape
    assert depth == 1, "single-layer model"
    a = n_heads * head_dim
    pw = state_pool.shape[-1]
    n_state = state_pool.shape[2]
    q_scale = head_dim ** -0.5
    li = 0

    n_pad = (-(bd + bp)) % 16
    c_all = jnp.concatenate([c_sample, c_prompt, jnp.zeros((n_pad, d), F32)], axis=0)
    mods = _ada(c_all, w_ada[li], b_ada[li][None, :])
    mods_p = mods[bd:bd + bp].reshape(bp, 1, N_MODS * d)
    tiles_per_seq = s_len // ROW_TILE

    def prompt_mod(k):
        return pl.BlockSpec((None, 1, d), lambda i: (i // tiles_per_seq, 0, k))

    def sample_mod(k):
        return pl.BlockSpec((bd, d), lambda i: (0, k))

    f1 = [w[li].astype(BF16) for w in (ffn1_w1, ffn1_w3, ffn1_w2)]
    f2 = [w[li].astype(BF16) for w in (ffn2_w1, ffn2_w3, ffn2_w2)]
    w_in_l = w_in[li]
    wqkv = w_in_l[:, :3 * a].astype(BF16)
    wqT = w_in_l[:, :a].T.astype(BF16)
    wkT = w_in_l[:, a:2 * a].T.astype(BF16)
    wvT = w_in_l[:, 2 * a:3 * a].T.astype(BF16)
    wf_cols = w_in_l[:, 3 * a:3 * a + n_heads]
    wf = jnp.pad(wf_cols, ((0, 0), (0, LANES - n_heads))).astype(BF16)
    wfT = wf_cols.T.astype(BF16)
    bf = jnp.pad(b_f[li][None, :], ((0, 0), (0, LANES - n_heads)))
    bfT = b_f[li][:, None]
    wu = w_in_l[:, 3 * a + n_heads:].astype(BF16)
    wo_a = w_o[li][:a].astype(BF16)
    wo_p = w_o[li][a:].astype(BF16)
    wp = w_pool[li].astype(BF16)
    ps = pool_scale[li][None, :]
    g1 = norm_ffn1[li][None, :]
    gm = norm_mix[li][None, :]
    g2 = norm_ffn2[li][None, :]
    gf = norm_final[None, :]

    xp = x_prompt.reshape(bp * s_len, d)
    xp = _ffn(xp, g1, mods_p, prompt_mod, 0, *f1)
    qTa_p, ka_p, vTb_p, kT_p, vT_p, lfT_p, u_p = _inproj_prompt(
        xp.reshape(bp, s_len, d), gm, mods_p, 3, wqT, wkT, wvT, wf, wfT, bf, bfT, wu,
        page, n_heads, head_dim, q_scale)
    attn_p = _attn_prompt(qTa_p, ka_p, vTb_p, head_dim)
    pool_p = _pool_prompt(u_p, wp, ps)
    xp = _outproj(xp, attn_p.reshape(bp * s_len, a), pool_p.reshape(bp * s_len, pw), mods_p, prompt_mod, 5, wo_a, wo_p)
    y_prompt = _ffn(xp, g2, mods_p, prompt_mod, 6, *f2, gf=gf).reshape(bp, s_len, d)

    n_pp = s_len // page
    new_k_prompt = kT_p.reshape(bp, n_pp, n_heads, head_dim, page).transpose(0, 1, 4, 2, 3)[None]
    new_v_prompt = vT_p.reshape(bp, n_pp, n_heads, head_dim, page).transpose(0, 1, 4, 2, 3)[None]
    new_logf_prompt = lfT_p.transpose(0, 1, 3, 2)[None]
    new_pool_prompt = u_p[:, s_len - n_state:][None]

    n_rows = bd * t_len
    xs = x_sample.transpose(1, 0, 2).reshape(n_rows, d)
    xs = _ffn(xs, g1, mods, sample_mod, 0, *f1)
    q_s, k_s, v_s, kT_s, vT_s, lfT_s, c_s, u_s = _inproj_sample(
        xs, gm, mods, sample_mod, 3, wqkv, wkT, wvT, wf, wfT, bf, bfT, wu, t_len, n_heads, q_scale)

    def per_seq(z):
        return z.reshape(t_len, bd, z.shape[-1]).transpose(1, 0, 2)

    q_rep = jnp.repeat(per_seq(q_s), n_heads, axis=1)
    c_bt = per_seq(c_s)
    cq = c_bt.reshape(bd, t_len * n_heads, 1)
    c2 = jnp.broadcast_to(c_bt.transpose(0, 2, 1)[:, None], (bd, t_len, n_heads, t_len)).reshape(bd, t_len * n_heads, t_len)
    cache_kT = cache_k[li].transpose(0, 2, 3, 1).reshape(n_phys, a, page)
    cache_vT = cache_v[li].transpose(0, 2, 3, 1).reshape(n_phys, a, page)
    cache_lfT = cache_logf[li].transpose(0, 2, 1)
    attn_s = _decode(page_table, q_rep, cq, c2, per_seq(k_s), per_seq(v_s), cache_kT, cache_vT, cache_lfT, head_dim)
    attn_s = attn_s.transpose(1, 0, 2).reshape(n_rows, a).astype(BF16)
    state_tm = state_pool[li].transpose(1, 0, 2)
    u_tm = u_s.reshape(t_len, bd, pw)
    pool_s = _pool_sample(state_tm, u_tm, wp, ps).reshape(n_rows, pw)
    xs = _outproj(xs, attn_s, pool_s, mods, sample_mod, 5, wo_a, wo_p)
    y_sample = _ffn(xs, g2, mods, sample_mod, 6, *f2, gf=gf).reshape(t_len, bd, d).transpose(1, 0, 2)

    new_k_sample = kT_s.reshape(t_len, n_heads, head_dim, bd).transpose(3, 0, 1, 2)[None]
    new_v_sample = vT_s.reshape(t_len, n_heads, head_dim, bd).transpose(3, 0, 1, 2)[None]
    new_logf_sample = lfT_s.transpose(2, 0, 1)[None]
    new_pool_sample = jnp.concatenate([state_tm[t_len:], u_tm], axis=0).transpose(1, 0, 2)[None]

    return (y_prompt, y_sample, new_k_prompt, new_v_prompt, new_logf_prompt, new_pool_prompt,
            new_k_sample, new_v_sample, new_logf_sample, new_pool_sample)
```

```python
import functools

import jax
import jax.numpy as jnp
from jax import lax
from jax.experimental import pallas as pl
from jax.experimental.pallas import tpu as pltpu

F32 = jnp.float32
BF16 = jnp.bfloat16

RMS_EPS = 1e-6
POOL_WINDOWS = (2, 4, 8, 16)
N_MODS = 9
LANES = 128
VMEM_LIMIT_BYTES = 56 * 1024 * 1024

ROW_TILE = 512
FFN_CHUNK = 256
ATTN_TILE = 512
DECODE_PAGES_PER_STEP = 16

_NT_DIMS = (((1,), (1,)), ((), ()))


def _dot(a, b):
    return jnp.dot(a, b, preferred_element_type=F32)


def _dot_nt(a, b):
    return lax.dot_general(a, b, _NT_DIMS, preferred_element_type=F32)


def _split3(x):
    hi = x.astype(BF16)
    r = x - hi.astype(F32)
    mid = r.astype(BF16)
    lo = (r - mid.astype(F32)).astype(BF16)
    return hi, mid, lo


def _split3_f32(x):
    hi, mid, lo = _split3(x)
    return hi.astype(F32), mid.astype(F32), lo.astype(F32)


def _dot_exact_left(x, ones_mat):
    hi, mid, lo = _split3(x)
    return _dot(hi, ones_mat) + _dot(mid, ones_mat) + _dot(lo, ones_mat)


def _dot_exact_right(ones_mat, x):
    hi, mid, lo = _split3(x)
    return _dot(ones_mat, hi) + _dot(ones_mat, mid) + _dot(ones_mat, lo)


def _silu(x):
    return x * jax.nn.sigmoid(x)


def _log_sigmoid(x):
    return jnp.minimum(x, 0.0) - jnp.log1p(jnp.exp(-jnp.abs(x)))


def _rmsnorm(x, g):
    return x * lax.rsqrt(jnp.mean(x * x, axis=-1, keepdims=True) + RMS_EPS) * g


def _rows(m, n):
    if m.shape[0] in (1, n):
        return m
    return jnp.concatenate([m] * (n // m.shape[0]), axis=0)


def _modnorm(x, g, shift, scale):
    n = x.shape[0]
    return _rmsnorm(x, g) * (1.0 + _rows(scale, n)) + _rows(shift, n)


def _resident(shape):
    zeros = (0,) * len(shape)
    return pl.BlockSpec(shape, lambda *_: zeros, pipeline_mode=pl.Buffered(1))


def _params():
    return pltpu.CompilerParams(vmem_limit_bytes=VMEM_LIMIT_BYTES)


def _ada_kernel(c_ref, w_ref, b_ref, o_ref):
    s = _silu(c_ref[...]).astype(BF16)
    o_ref[...] = _dot(s, w_ref[...].astype(BF16)) + b_ref[...]


def _ada(c, w, b):
    n, d = c.shape
    m = w.shape[1]
    tn = d
    return pl.pallas_call(
        _ada_kernel,
        grid=(m // tn,),
        in_specs=[
            pl.BlockSpec((n, d), lambda j: (0, 0)),
            pl.BlockSpec((d, tn), lambda j: (0, j)),
            pl.BlockSpec((1, tn), lambda j: (0, j)),
        ],
        out_specs=pl.BlockSpec((n, tn), lambda j: (0, j)),
        out_shape=jax.ShapeDtypeStruct((n, m), F32),
        compiler_params=_params(),
        name="ada_params",
    )(c, w, b)


def _ffn_begin(x_ref, g_ref, sh_ref, sc_ref, h_ref, acc_ref):
    h_ref[...] = _modnorm(x_ref[...], g_ref[...], sh_ref[...], sc_ref[...]).astype(BF16)
    acc_ref[...] = jnp.zeros_like(acc_ref)


def _ffn_chunk(c, h_ref, acc_ref, w1_ref, w3_ref, w2_ref):
    off = pl.multiple_of(c * FFN_CHUNK, FFN_CHUNK)
    h = h_ref[...]
    a = _dot(h, w1_ref[:, pl.ds(off, FFN_CHUNK)])
    b = _dot(h, w3_ref[:, pl.ds(off, FFN_CHUNK)])
    act = (_silu(a) * b).astype(BF16)
    acc_ref[...] += _dot(act, w2_ref[pl.ds(off, FFN_CHUNK), :])


def _ffn_end(x_ref, gt_ref, acc_ref):
    x = x_ref[...]
    return x + 0.5 * _rows(gt_ref[...], x.shape[0]) * acc_ref[...]


def _ffn_kernel(x_ref, g_ref, sh_ref, sc_ref, gt_ref, w1_ref, w3_ref, w2_ref, *rest, final_norm):
    if final_norm:
        gf_ref, o_ref, h_ref, acc_ref = rest
    else:
        o_ref, h_ref, acc_ref = rest
    _ffn_begin(x_ref, g_ref, sh_ref, sc_ref, h_ref, acc_ref)

    def chunk(c, carry):
        _ffn_chunk(c, h_ref, acc_ref, w1_ref, w3_ref, w2_ref)
        return carry

    lax.fori_loop(0, w1_ref.shape[1] // FFN_CHUNK, chunk, 0, unroll=True)
    out = _ffn_end(x_ref, gt_ref, acc_ref)
    if final_norm:
        out = _rmsnorm(out, gf_ref[...])
    o_ref[...] = out


def _ffn(x, g, mods, mod_spec, k0, w1, w3, w2, gf=None):
    r, d = x.shape
    dff = w1.shape[1]
    tm = ROW_TILE
    in_specs = [
        pl.BlockSpec((tm, d), lambda i: (i, 0)),
        _resident((1, d)),
        mod_spec(k0),
        mod_spec(k0 + 1),
        mod_spec(k0 + 2),
        _resident((d, dff)),
        _resident((d, dff)),
        _resident((dff, d)),
    ]
    args = [x, g, mods, mods, mods, w1, w3, w2]
    if gf is not None:
        in_specs.append(_resident((1, d)))
        args.append(gf)
    return pl.pallas_call(
        functools.partial(_ffn_kernel, final_norm=gf is not None),
        grid=(r // tm,),
        in_specs=in_specs,
        out_specs=pl.BlockSpec((tm, d), lambda i: (i, 0)),
        out_shape=jax.ShapeDtypeStruct((r, d), F32),
        scratch_shapes=[pltpu.VMEM((tm, d), BF16), pltpu.VMEM((tm, d), F32)],
        compiler_params=_params(),
        name="ffn_final" if gf is not None else "ffn",
    )(*args)


N_BIAS_PARTS = 3
LOG2_E = 1.4426950408889634
PV_SUM_ROWS = 16


def _inproj_prompt_kernel(x_ref, g_ref, sh_ref, sc_ref, wqT_ref, wkT_ref, wvT_ref, wf_ref, wfT_ref,
                          bf_ref, bfT_ref, wu_ref,
                          qTa_ref, ka_ref, vTa_ref, kT_ref, vT_ref, lfT_ref, u_ref,
                          carry_ref, carryT_ref, *, q_scale, head_dim):
    @pl.when(pl.program_id(1) == 0)
    def _():
        carry_ref[...] = jnp.zeros_like(carry_ref)
        carryT_ref[...] = jnp.zeros_like(carryT_ref)

    tm = x_ref.shape[0]
    page = kT_ref.shape[-1]
    n_heads, depth, _ = qTa_ref.shape
    h = _modnorm(x_ref[...], g_ref[...], sh_ref[...], sc_ref[...]).astype(BF16)
    u_ref[...] = _dot(h, wu_ref[...])
    kT = _dot_nt(wkT_ref[...], h)
    vT = _dot_nt(wvT_ref[...], h)
    lf = _log_sigmoid(_dot(h, wf_ref[...]) + bf_ref[...])
    lfT = _log_sigmoid(_dot_nt(wfT_ref[...], h) + bfT_ref[...])
    for j in range(tm // page):
        cols = slice(j * page, (j + 1) * page)
        kT_ref[j] = kT[:, cols]
        vT_ref[j] = vT[:, cols]
        lfT_ref[j] = lfT[:, cols]
    r = lax.broadcasted_iota(jnp.int32, (tm, tm), 0)
    c = lax.broadcasted_iota(jnp.int32, (tm, tm), 1)
    cf = _dot_exact_right((c <= r).astype(BF16), lf) + carry_ref[...]
    carry_ref[...] = cf[tm - 1:tm, :]
    cfT = _dot_exact_left(lfT, (r <= c).astype(BF16)) + carryT_ref[:, :1]
    carryT_ref[...] = jnp.broadcast_to(cfT[:, tm - 1:tm], carryT_ref.shape)
    qT = _dot_nt(wqT_ref[...], h) * (q_scale * LOG2_E)
    k = kT.T
    sub = lax.broadcasted_iota(jnp.int32, (8, tm), 0)
    lane = lax.broadcasted_iota(jnp.int32, (tm, depth), 1)
    pad_rows = jnp.zeros((depth - head_dim - 8, tm), F32)
    sum_rows = vTa_ref.shape[1] - head_dim
    ones_row = jnp.where(lax.broadcasted_iota(jnp.int32, (sum_rows, tm), 0) == 0, 1.0, 0.0)
    for hd in range(n_heads):
        rows = slice(hd * head_dim, (hd + 1) * head_dim)
        vTa_ref[hd] = jnp.concatenate([vT[rows, :], ones_row], axis=0).astype(BF16)
        f_hi, f_mid, f_lo = _split3_f32(jnp.broadcast_to(cfT[hd:hd + 1, :] * LOG2_E, (8, tm)))
        q_bias = jnp.where(sub == 0, f_hi, jnp.where(sub == 1, f_mid, jnp.where(sub == 2, f_lo,
                 jnp.where(sub < 2 * N_BIAS_PARTS, 1.0, 0.0))))
        qTa_ref[hd] = jnp.concatenate([qT[rows, :], q_bias, pad_rows], axis=0).astype(BF16)
        n_hi, n_mid, n_lo = _split3_f32(jnp.broadcast_to(cf[:, hd:hd + 1] * -LOG2_E, (tm, depth)))
        k_bias = jnp.where(lane == head_dim + 3, n_hi, jnp.where(lane == head_dim + 4, n_mid,
                 jnp.where(lane == head_dim + 5, n_lo, jnp.where(lane < head_dim + N_BIAS_PARTS, 1.0, 0.0))))
        blk = k[:, (hd // 2) * depth:(hd // 2 + 1) * depth]
        if hd % 2:
            blk = pltpu.roll(blk, head_dim, 1)
        ka_ref[hd] = jnp.where(lane < head_dim, blk, k_bias).astype(BF16)


def _inproj_prompt(x, g, mods, k0, wqT, wkT, wvT, wf, wfT, bf, bfT, wu, page, n_heads, head_dim, q_scale):
    b, s, d = x.shape
    a = wkT.shape[0]
    pw = wu.shape[1]
    tm = ROW_TILE
    npg = tm // page
    depth = 2 * head_dim
    assert depth == LANES and depth - head_dim >= 8

    def mod_spec(k):
        return pl.BlockSpec((None, 1, d), lambda bi, ti: (bi, 0, k))

    out_shape = (
        jax.ShapeDtypeStruct((b, n_heads, depth, s), BF16),
        jax.ShapeDtypeStruct((b, n_heads, s, depth), BF16),
        jax.ShapeDtypeStruct((b, n_heads, head_dim + PV_SUM_ROWS, s), BF16),
        jax.ShapeDtypeStruct((b, s // page, a, page), F32),
        jax.ShapeDtypeStruct((b, s // page, a, page), F32),
        jax.ShapeDtypeStruct((b, s // page, n_heads, page), F32),
        jax.ShapeDtypeStruct((b, s, pw), F32),
    )
    out_specs = (
        pl.BlockSpec((None, n_heads, depth, tm), lambda bi, ti: (bi, 0, 0, ti)),
        pl.BlockSpec((None, n_heads, tm, depth), lambda bi, ti: (bi, 0, ti, 0)),
        pl.BlockSpec((None, n_heads, head_dim + PV_SUM_ROWS, tm), lambda bi, ti: (bi, 0, 0, ti)),
        pl.BlockSpec((None, npg, a, page), lambda bi, ti: (bi, ti, 0, 0)),
        pl.BlockSpec((None, npg, a, page), lambda bi, ti: (bi, ti, 0, 0)),
        pl.BlockSpec((None, npg, n_heads, page), lambda bi, ti: (bi, ti, 0, 0)),
        pl.BlockSpec((None, tm, pw), lambda bi, ti: (bi, ti, 0)),
    )
    weights = (wqT, wkT, wvT, wf, wfT, bf, bfT, wu)
    in_specs = [
        pl.BlockSpec((None, tm, d), lambda bi, ti: (bi, ti, 0)),
        _resident((1, d)),
        mod_spec(k0),
        mod_spec(k0 + 1),
    ] + [_resident(w.shape) for w in weights]
    return pl.pallas_call(
        functools.partial(_inproj_prompt_kernel, q_scale=q_scale, head_dim=head_dim),
        grid=(b, s // tm),
        in_specs=in_specs,
        out_specs=out_specs,
        out_shape=out_shape,
        scratch_shapes=[pltpu.VMEM((1, LANES), F32), pltpu.VMEM((n_heads, LANES), F32)],
        compiler_params=_params(),
        name="inproj_prompt",
    )(x, g, mods, mods, *weights)


def _attn_prompt_kernel(qTa_ref, ka_ref, vTa_ref, o_ref, m_ref, acc_ref, s_ref, *, head_dim):
    qi = pl.program_id(2)
    tq = qTa_ref.shape[-1]
    tk = tq
    m_ref[...] = jnp.full_like(m_ref, -jnp.inf)
    acc_ref[...] = jnp.zeros_like(acc_ref)

    def scores_into(slot, j):
        off = pl.multiple_of(j * tk, tk)
        for i in range(2):
            s_ref[slot, i] = _dot(ka_ref[i, pl.ds(off, tk), :], qTa_ref[i])

    def consume(slot, j, masked):
        off = pl.multiple_of(j * tk, tk)
        for i in range(2):
            s = s_ref[slot, i]
            if masked:
                key = lax.broadcasted_iota(jnp.int32, (tk, tq), 0)
                qry = lax.broadcasted_iota(jnp.int32, (tk, tq), 1)
                s = jnp.where(key <= qry, s, -jnp.inf)
            m_prev = m_ref[i]
            m_new = jnp.maximum(m_prev, jnp.max(s, axis=0, keepdims=True))
            alpha = jnp.exp2(m_prev - m_new)
            p = jnp.exp2(s - m_new)
            acc_ref[i] = alpha * acc_ref[i] + _dot(vTa_ref[i, :, pl.ds(off, tk)], p.astype(BF16))
            m_ref[i] = m_new

    def two_full_blocks(jj, carry):
        scores_into(1, 2 * jj + 1)
        consume(0, 2 * jj, False)
        scores_into(0, 2 * jj + 2)
        consume(1, 2 * jj + 1, False)
        return carry

    scores_into(0, 0)
    lax.fori_loop(0, qi // 2, two_full_blocks, 0)

    @pl.when(qi % 2 == 0)
    def _():
        consume(0, qi, True)

    @pl.when(qi % 2 == 1)
    def _():
        scores_into(1, qi)
        consume(0, qi - 1, False)
        consume(1, qi, True)

    oT = jnp.concatenate(
        [acc_ref[i, :head_dim, :] / acc_ref[i, head_dim:head_dim + 1, :] for i in range(2)], axis=0)
    o_ref[...] = oT.T.astype(BF16)


def _attn_prompt(qTa, ka, vTa, head_dim):
    b, n_heads, depth, s = qTa.shape
    v_rows = vTa.shape[2]
    a = n_heads * head_dim
    tq = ATTN_TILE
    return pl.pallas_call(
        functools.partial(_attn_prompt_kernel, head_dim=head_dim),
        grid=(b, n_heads // 2, s // tq),
        in_specs=[
            pl.BlockSpec((None, 2, depth, tq), lambda bi, hp, qi: (bi, hp, 0, qi)),
            pl.BlockSpec((None, 2, s, depth), lambda bi, hp, qi: (bi, hp, 0, 0)),
            pl.BlockSpec((None, 2, v_rows, s), lambda bi, hp, qi: (bi, hp, 0, 0)),
        ],
        out_specs=pl.BlockSpec((None, tq, depth), lambda bi, hp, qi: (bi, qi, hp)),
        out_shape=jax.ShapeDtypeStruct((b, s, a), BF16),
        scratch_shapes=[
            pltpu.VMEM((2, 1, tq), F32),
            pltpu.VMEM((2, v_rows, tq), F32),
            pltpu.VMEM((2, 2, tq, tq), F32),
        ],
        compiler_params=_params(),
        name="attn_prompt",
    )(qTa, ka, vTa)


def _pool_prompt_kernel(u_ref, wp_ref, ps_ref, o_ref, halo_ref):
    ti = pl.program_id(1)

    @pl.when(ti == 0)
    def _():
        halo_ref[...] = jnp.zeros_like(halo_ref)

    tm = u_ref.shape[0]
    nh = halo_ref.shape[0]
    gd = wp_ref.shape[1]
    u = u_ref[...]
    ext = jnp.concatenate([halo_ref[...], u], axis=0)
    halo_ref[...] = u[tm - nh:, :]
    sums = {1: ext}
    w = 1
    while w < max(POOL_WINDOWS):
        sums[2 * w] = sums[w] + pltpu.roll(sums[w], w, 0)
        w *= 2
    pos = ti * tm + lax.broadcasted_iota(jnp.int32, (tm, 1), 0)
    ys = []
    for gi, w in enumerate(POOL_WINDOWS):
        lanes = slice(gi * gd, (gi + 1) * gd)
        count = jnp.minimum(pos + 1, w).astype(F32)
        d = sums[w][nh:, lanes] / count - u[:, lanes]
        ys.append(_dot(d.astype(BF16), wp_ref[gi]))
    o_ref[...] = (jnp.concatenate(ys, axis=1) * ps_ref[...]).astype(BF16)


def _pool_prompt(u, wp, ps):
    b, s, pw = u.shape
    tm = ROW_TILE
    halo = 16
    assert halo >= max(POOL_WINDOWS)
    return pl.pallas_call(
        _pool_prompt_kernel,
        grid=(b, s // tm),
        in_specs=[
            pl.BlockSpec((None, tm, pw), lambda bi, ti: (bi, ti, 0)),
            _resident(wp.shape),
            _resident(ps.shape),
        ],
        out_specs=pl.BlockSpec((None, tm, pw), lambda bi, ti: (bi, ti, 0)),
        out_shape=jax.ShapeDtypeStruct((b, s, pw), BF16),
        scratch_shapes=[pltpu.VMEM((halo, pw), F32)],
        compiler_params=_params(),
        name="pool_prompt",
    )(u, wp, ps)


def _outproj_kernel(x_ref, a_ref, p_ref, gt_ref, wa_ref, wp_ref, o_ref):
    x = x_ref[...]
    mix = _dot(a_ref[...], wa_ref[...]) + _dot(p_ref[...], wp_ref[...])
    o_ref[...] = x + _rows(gt_ref[...], x.shape[0]) * mix


def _outproj(x, attn, pool, mods, mod_spec, k, wa, wp):
    r, d = x.shape
    tm = ROW_TILE
    return pl.pallas_call(
        _outproj_kernel,
        grid=(r // tm,),
        in_specs=[
            pl.BlockSpec((tm, d), lambda i: (i, 0)),
            pl.BlockSpec((tm, attn.shape[1]), lambda i: (i, 0)),
            pl.BlockSpec((tm, pool.shape[1]), lambda i: (i, 0)),
            mod_spec(k),
            _resident(wa.shape),
            _resident(wp.shape),
        ],
        out_specs=pl.BlockSpec((tm, d), lambda i: (i, 0)),
        out_shape=jax.ShapeDtypeStruct((r, d), F32),
        compiler_params=_params(),
        name="outproj",
    )(x, attn, pool, mods, wa, wp)


def _inproj_sample_kernel(x_ref, g_ref, sh_ref, sc_ref, wqkv_ref, wkT_ref, wvT_ref, wf_ref, wfT_ref,
                          bf_ref, bfT_ref, wu_ref,
                          q_ref, k_ref, v_ref, kT_ref, vT_ref, lfT_ref, c_ref, u_ref, *, q_scale):
    n_t, _, nb = kT_ref.shape
    n_heads = c_ref.shape[-1]
    h = _modnorm(x_ref[...], g_ref[...], sh_ref[...], sc_ref[...]).astype(BF16)
    z = _dot(h, wqkv_ref[...])
    a = z.shape[1] // 3
    q_ref[...] = (z[:, :a] * q_scale).astype(BF16)
    k_ref[...] = z[:, a:2 * a]
    v_ref[...] = z[:, 2 * a:]
    u_ref[...] = _dot(h, wu_ref[...])
    kT = _dot_nt(wkT_ref[...], h)
    vT = _dot_nt(wvT_ref[...], h)
    lf = _log_sigmoid(_dot(h, wf_ref[...]) + bf_ref[...])
    lfT = _log_sigmoid(_dot_nt(wfT_ref[...], h) + bfT_ref[...])
    run = None
    parts = []
    for t in range(n_t):
        cols = slice(t * nb, (t + 1) * nb)
        kT_ref[t] = kT[:, cols]
        vT_ref[t] = vT[:, cols]
        lfT_ref[t] = lfT[:, cols]
        blk = lf[t * nb:(t + 1) * nb, :]
        run = blk if run is None else run + blk
        parts.append(run)
    c_ref[...] = jnp.concatenate(parts, axis=0)[:, :n_heads]


def _inproj_sample(x, g, mods, mod_spec, k0, wqkv, wkT, wvT, wf, wfT, bf, bfT, wu, n_t, n_heads, q_scale):
    r, d = x.shape
    a = wkT.shape[0]
    pw = wu.shape[1]
    nb = r // n_t
    out_shape = (
        jax.ShapeDtypeStruct((r, a), BF16),
        jax.ShapeDtypeStruct((r, a), F32),
        jax.ShapeDtypeStruct((r, a), F32),
        jax.ShapeDtypeStruct((n_t, a, nb), F32),
        jax.ShapeDtypeStruct((n_t, a, nb), F32),
        jax.ShapeDtypeStruct((n_t, n_heads, nb), F32),
        jax.ShapeDtypeStruct((r, n_heads), F32),
        jax.ShapeDtypeStruct((r, pw), F32),
    )
    whole = lambda shp: pl.BlockSpec(shp, lambda i: (0,) * len(shp))
    in_specs = [whole((r, d)), whole((1, d)), mod_spec(k0), mod_spec(k0 + 1)] + [
        whole(w.shape) for w in (wqkv, wkT, wvT, wf, wfT, bf, bfT, wu)
    ]
    return pl.pallas_call(
        functools.partial(_inproj_sample_kernel, q_scale=q_scale),
        grid=(1,),
        in_specs=in_specs,
        out_specs=tuple(whole(o.shape) for o in out_shape),
        out_shape=out_shape,
        compiler_params=_params(),
        name="inproj_sample",
    )(x, g, mods, mods, wqkv, wkT, wvT, wf, wfT, bf, bfT, wu)


def _decode_kernel(pt_ref, q_ref, cq_ref, c2_ref, kn_ref, vn_ref, kc_ref, vc_ref, lfc_ref,
                   x_ref, g_ref, sh_ref, sc_ref, gt_ref, w1_ref, w3_ref, w2_ref,
                   o_ref, y_ref,
                   kbuf, vbuf, lfbuf, sems, m_ref, l_ref, acc_ref, r_ref, h_ref, facc_ref, *, head_dim, ffn_tiles):
    g_n = kbuf.shape[1]
    seq = pl.program_id(0)
    step = pl.program_id(1)
    n_steps = pl.num_programs(1)
    n_pages = n_steps * g_n
    lin = seq * n_steps + step
    slot = lin % 2

    def page_copies(seq_i, step_i, slot_i):
        copies = []
        for i in range(g_n):
            pid = pt_ref[seq_i, n_pages - 1 - (step_i * g_n + i)]
            copies.append(pltpu.make_async_copy(kc_ref.at[pid], kbuf.at[slot_i, i], sems.at[slot_i, 0, i]))
            copies.append(pltpu.make_async_copy(vc_ref.at[pid], vbuf.at[slot_i, i], sems.at[slot_i, 1, i]))
            copies.append(pltpu.make_async_copy(lfc_ref.at[pid], lfbuf.at[slot_i, i], sems.at[slot_i, 2, i]))
        return copies

    @pl.when(lin == 0)
    def _():
        for cp in page_copies(0, 0, 0):
            cp.start()

    @pl.when(lin + 1 < pl.num_programs(0) * n_steps)
    def _():
        wrap = step + 1 == n_steps
        for cp in page_copies(jnp.where(wrap, seq + 1, seq), jnp.where(wrap, 0, step + 1), 1 - slot):
            cp.start()

    n_chunks = w1_ref.shape[1] // FFN_CHUNK
    chunk = lin % n_chunks

    @pl.when(lin < ffn_tiles * n_chunks)
    def _():
        @pl.when(chunk == 0)
        def _():
            _ffn_begin(x_ref, g_ref, sh_ref, sc_ref, h_ref, facc_ref)

        _ffn_chunk(chunk, h_ref, facc_ref, w1_ref, w3_ref, w2_ref)

        @pl.when(chunk == n_chunks - 1)
        def _():
            y_ref[...] = _ffn_end(x_ref, gt_ref, facc_ref)

    for cp in page_copies(seq, step, slot):
        cp.wait()
    k_refs = [kbuf.at[slot, i] for i in range(g_n)]
    v_refs = [vbuf.at[slot, i] for i in range(g_n)]
    lf_refs = [lfbuf.at[slot, i] for i in range(g_n)]
    n_rows, width = q_ref.shape
    n_heads = width // head_dim
    n_t = n_rows // n_heads
    row = lax.broadcasted_iota(jnp.int32, (n_rows, 1), 0)
    lane = lax.broadcasted_iota(jnp.int32, (1, width), 1)
    own = (lane // head_dim) == (row % n_heads)
    t_row = row // n_heads
    q = q_ref[...]
    qbd = jnp.where(own, q, jnp.zeros_like(q))
    cq = cq_ref[...]

    @pl.when(step == 0)
    def _():
        qf = qbd.astype(F32)
        kn = kn_ref[...]
        vn = vn_ref[...]
        c2 = c2_ref[...]
        sc = []
        for s in range(n_t):
            col = jnp.sum(qf * kn[s:s + 1, :], axis=1, keepdims=True) + cq - c2[:, s:s + 1]
            sc.append(jnp.where(t_row >= s, col, -jnp.inf))
        m0 = functools.reduce(jnp.maximum, sc)
        l0 = jnp.zeros_like(m0)
        a0 = jnp.zeros(acc_ref.shape, F32)
        for s in range(n_t):
            p = jnp.exp(sc[s] - m0)
            l0 = l0 + p
            a0 = a0 + p * vn[s:s + 1, :]
        m_ref[...] = m0
        l_ref[...] = l0
        acc_ref[...] = a0
        r_ref[...] = jnp.zeros_like(r_ref)

    page = lf_refs[0].shape[-1]
    pj = lax.broadcasted_iota(jnp.int32, (page, page), 0)
    ps = lax.broadcasted_iota(jnp.int32, (page, page), 1)
    later = (pj > ps).astype(BF16)
    r_run = r_ref[...]
    lf_all = jnp.concatenate([lf_refs[i][...] for i in range(g_n)], axis=0)
    suffix_all = _dot_exact_left(lf_all, later)
    totals = jnp.sum(lf_all, axis=1, keepdims=True)
    scores = []
    for i in range(g_n):
        rows = slice(i * n_heads, (i + 1) * n_heads)
        suffix = suffix_all[rows] + r_run
        r_run = r_run + totals[rows]
        bias = jnp.concatenate([suffix] * n_t, axis=0) + cq
        scores.append(_dot(qbd, k_refs[i][...].astype(BF16)) + bias)
    r_ref[...] = r_run
    m_prev = m_ref[...]
    m_new = functools.reduce(jnp.maximum, [m_prev] + [jnp.max(s, axis=1, keepdims=True) for s in scores])
    alpha = jnp.exp(m_prev - m_new)
    l_new = alpha * l_ref[...]
    acc = alpha * acc_ref[...]
    for i in range(g_n):
        p = jnp.exp(scores[i] - m_new)
        l_new = l_new + jnp.sum(p, axis=1, keepdims=True)
        acc = acc + _dot_nt(p.astype(BF16), v_refs[i][...].astype(BF16))
    m_ref[...] = m_new
    l_ref[...] = l_new
    acc_ref[...] = acc

    @pl.when(step == pl.num_programs(1) - 1)
    def _():
        o = jnp.where(own, acc, 0.0) / l_new
        o_ref[...] = o.reshape(n_t, n_heads, width).sum(axis=1)


def _decode_and_ffn(page_table, q_rep, cq, c2, k_new, v_new, cache_kT, cache_vT, cache_lfT, head_dim,
                    x, g, mods, k0, tiles_per_seq, w1, w3, w2):
    bd, n_pages = page_table.shape
    _, n_rows, width = q_rep.shape
    n_t = k_new.shape[1]
    page = cache_kT.shape[-1]
    n_heads = cache_lfT.shape[1]
    g_n = DECODE_PAGES_PER_STEP
    n_steps = n_pages // g_n
    r, d = x.shape
    dff = w1.shape[1]
    tm = ROW_TILE
    ffn_tiles = r // tm
    n_chunks = dff // FFN_CHUNK
    assert ffn_tiles * n_chunks <= bd * n_steps, "not enough decode steps to carry the SwiGLU chunks"

    def tile(b, s):
        return jnp.minimum((b * n_steps + s) // n_chunks, ffn_tiles - 1)

    def mod_spec(k):
        return pl.BlockSpec((None, 1, d), lambda b, s, pt: (tile(b, s) // tiles_per_seq, 0, k))

    per_seq = lambda shp: pl.BlockSpec((None,) + shp, lambda b, s, pt: (b, 0, 0))
    in_hbm = pl.BlockSpec(memory_space=pl.ANY)
    row_tile = pl.BlockSpec((tm, d), lambda b, s, pt: (tile(b, s), 0))
    in_specs = [
        per_seq((n_rows, width)),
        per_seq((n_rows, 1)),
        per_seq((n_rows, n_t)),
        per_seq((n_t, width)),
        per_seq((n_t, width)),
        in_hbm,
        in_hbm,
        in_hbm,
        row_tile,
        _resident((1, d)),
        mod_spec(k0),
        mod_spec(k0 + 1),
        mod_spec(k0 + 2),
        _resident((d, dff)),
        _resident((d, dff)),
        _resident((dff, d)),
    ]
    grid_spec = pltpu.PrefetchScalarGridSpec(
        num_scalar_prefetch=1,
        grid=(bd, n_steps),
        in_specs=in_specs,
        out_specs=(per_seq((n_t, width)), row_tile),
        scratch_shapes=[
            pltpu.VMEM((2, g_n, width, page), F32),
            pltpu.VMEM((2, g_n, width, page), F32),
            pltpu.VMEM((2, g_n, n_heads, page), F32),
            pltpu.SemaphoreType.DMA((2, 3, g_n)),
            pltpu.VMEM((n_rows, 1), F32),
            pltpu.VMEM((n_rows, 1), F32),
            pltpu.VMEM((n_rows, width), F32),
            pltpu.VMEM((n_heads, LANES), F32),
            pltpu.VMEM((tm, d), BF16),
            pltpu.VMEM((tm, d), F32),
        ],
    )
    return pl.pallas_call(
        functools.partial(_decode_kernel, head_dim=head_dim, ffn_tiles=ffn_tiles),
        grid_spec=grid_spec,
        out_shape=(jax.ShapeDtypeStruct((bd, n_t, width), F32), jax.ShapeDtypeStruct((r, d), F32)),
        compiler_params=_params(),
        name="decode_attn_ffn",
    )(page_table, q_rep, cq, c2, k_new, v_new, cache_kT, cache_vT, cache_lfT, x, g, mods, mods, mods, w1, w3, w2)


def _pool_sample_kernel(st_ref, u_ref, wp_ref, ps_ref, o_ref):
    n_state = st_ref.shape[0]
    n_t = u_ref.shape[0]
    gd = wp_ref.shape[1]

    def rows(idx, lanes):
        return st_ref[idx, :, lanes] if idx < n_state else u_ref[idx - n_state, :, lanes]

    for t in range(n_t):
        ys = []
        for gi, w in enumerate(POOL_WINDOWS):
            lanes = slice(gi * gd, (gi + 1) * gd)
            cur = rows(n_state + t, lanes)
            total = cur
            for j in range(1, w):
                total = total + rows(n_state + t - j, lanes)
            d = total / float(w) - cur
            ys.append(_dot(d.astype(BF16), wp_ref[gi]))
        o_ref[t] = (jnp.concatenate(ys, axis=1) * ps_ref[...]).astype(BF16)


def _pool_sample(state_tm, u_tm, wp, ps):
    assert state_tm.shape[0] >= max(POOL_WINDOWS) - 1
    whole = lambda shp: pl.BlockSpec(shp, lambda i: (0,) * len(shp))
    return pl.pallas_call(
        _pool_sample_kernel,
        grid=(1,),
        in_specs=[whole(state_tm.shape), whole(u_tm.shape), whole(wp.shape), whole(ps.shape)],
        out_specs=whole(u_tm.shape),
        out_shape=jax.ShapeDtypeStruct(u_tm.shape, BF16),
        compiler_params=_params(),
        name="pool_sample",
    )(state_tm, u_tm, wp, ps)


def kernel(x_prompt, x_sample, cache_k, cache_v, cache_logf, state_pool, page_table, c_prompt, c_sample, norm_ffn1, norm_mix, norm_ffn2, norm_final, w_ada, b_ada, ffn1_w1, ffn1_w3, ffn1_w2, ffn2_w1, ffn2_w3, ffn2_w2, w_in, b_f, w_o, w_pool, pool_scale):
    bp, s_len, d = x_prompt.shape
    bd, t_len, _ = x_sample.shape
    depth, n_phys, page, n_heads, head_dim = cache_k.shape
    assert depth == 1, "single-layer model"
    a = n_heads * head_dim
    pw = state_pool.shape[-1]
    n_state = state_pool.shape[2]
    q_scale = head_dim ** -0.5
    li = 0

    n_pad = (-(bd + bp)) % 16
    c_all = jnp.concatenate([c_sample, c_prompt, jnp.zeros((n_pad, d), F32)], axis=0)
    mods = _ada(c_all, w_ada[li], b_ada[li][None, :])
    mods_p = mods[bd:bd + bp].reshape(bp, 1, N_MODS * d)
    tiles_per_seq = s_len // ROW_TILE

    def prompt_mod(k):
        return pl.BlockSpec((None, 1, d), lambda i: (i // tiles_per_seq, 0, k))

    def sample_mod(k):
        return pl.BlockSpec((bd, d), lambda i: (0, k))

    f1 = [w[li].astype(BF16) for w in (ffn1_w1, ffn1_w3, ffn1_w2)]
    f2 = [w[li].astype(BF16) for w in (ffn2_w1, ffn2_w3, ffn2_w2)]
    w_in_l = w_in[li]
    wqkv = w_in_l[:, :3 * a].astype(BF16)
    wqT = w_in_l[:, :a].T.astype(BF16)
    wkT = w_in_l[:, a:2 * a].T.astype(BF16)
    wvT = w_in_l[:, 2 * a:3 * a].T.astype(BF16)
    wf_cols = w_in_l[:, 3 * a:3 * a + n_heads]
    wf = jnp.pad(wf_cols, ((0, 0), (0, LANES - n_heads))).astype(BF16)
    wfT = wf_cols.T.astype(BF16)
    bf = jnp.pad(b_f[li][None, :], ((0, 0), (0, LANES - n_heads)))
    bfT = b_f[li][:, None]
    wu = w_in_l[:, 3 * a + n_heads:].astype(BF16)
    wo_a = w_o[li][:a].astype(BF16)
    wo_p = w_o[li][a:].astype(BF16)
    wp = w_pool[li].astype(BF16)
    ps = pool_scale[li][None, :]
    g1 = norm_ffn1[li][None, :]
    gm = norm_mix[li][None, :]
    g2 = norm_ffn2[li][None, :]
    gf = norm_final[None, :]

    n_rows = bd * t_len
    xs = x_sample.transpose(1, 0, 2).reshape(n_rows, d)
    xs = _ffn(xs, g1, mods, sample_mod, 0, *f1)
    q_s, k_s, v_s, kT_s, vT_s, lfT_s, c_s, u_s = _inproj_sample(
        xs, gm, mods, sample_mod, 3, wqkv, wkT, wvT, wf, wfT, bf, bfT, wu, t_len, n_heads, q_scale)

    def per_seq(z):
        return z.reshape(t_len, bd, z.shape[-1]).transpose(1, 0, 2)

    q_rep = jnp.repeat(per_seq(q_s), n_heads, axis=1)
    c_bt = per_seq(c_s)
    cq = c_bt.reshape(bd, t_len * n_heads, 1)
    c2 = jnp.broadcast_to(c_bt.transpose(0, 2, 1)[:, None], (bd, t_len, n_heads, t_len)).reshape(bd, t_len * n_heads, t_len)
    cache_kT = cache_k[li].transpose(0, 2, 3, 1).reshape(n_phys, a, page)
    cache_vT = cache_v[li].transpose(0, 2, 3, 1).reshape(n_phys, a, page)
    cache_lfT = cache_logf[li].transpose(0, 2, 1)

    attn_s, xp = _decode_and_ffn(
        page_table, q_rep, cq, c2, per_seq(k_s), per_seq(v_s), cache_kT, cache_vT, cache_lfT, head_dim,
        x_prompt.reshape(bp * s_len, d), g1, mods_p, 0, tiles_per_seq, *f1)

    qTa_p, ka_p, vTa_p, kT_p, vT_p, lfT_p, u_p = _inproj_prompt(
        xp.reshape(bp, s_len, d), gm, mods_p, 3, wqT, wkT, wvT, wf, wfT, bf, bfT, wu,
        page, n_heads, head_dim, q_scale)
    attn_p = _attn_prompt(qTa_p, ka_p, vTa_p, head_dim)
    pool_p = _pool_prompt(u_p, wp, ps)
    xp = _outproj(xp, attn_p.reshape(bp * s_len, a), pool_p.reshape(bp * s_len, pw), mods_p, prompt_mod, 5, wo_a, wo_p)
    y_prompt = _ffn(xp, g2, mods_p, prompt_mod, 6, *f2, gf=gf).reshape(bp, s_len, d)

    n_pp = s_len // page
    new_k_prompt = kT_p.reshape(bp, n_pp, n_heads, head_dim, page).transpose(0, 1, 4, 2, 3)[None]
    new_v_prompt = vT_p.reshape(bp, n_pp, n_heads, head_dim, page).transpose(0, 1, 4, 2, 3)[None]
    new_logf_prompt = lfT_p.transpose(0, 1, 3, 2)[None]
    new_pool_prompt = u_p[:, s_len - n_state:][None]

    attn_s = attn_s.transpose(1, 0, 2).reshape(n_rows, a).astype(BF16)
    state_tm = state_pool[li].transpose(1, 0, 2)
    u_tm = u_s.reshape(t_len, bd, pw)
    pool_s = _pool_sample(state_tm, u_tm, wp, ps).reshape(n_rows, pw)
    xs = _outproj(xs, attn_s, pool_s, mods, sample_mod, 5, wo_a, wo_p)
    y_sample = _ffn(xs, g2, mods, sample_mod, 6, *f2, gf=gf).reshape(t_len, bd, d).transpose(1, 0, 2)

    new_k_sample = kT_s.reshape(t_len, n_heads, head_dim, bd).transpose(3, 0, 1, 2)[None]
    new_v_sample = vT_s.reshape(t_len, n_heads, head_dim, bd).transpose(3, 0, 1, 2)[None]
    new_logf_sample = lfT_s.transpose(2, 0, 1)[None]
    new_pool_sample = jnp.concatenate([state_tm[t_len:], u_tm], axis=0).transpose(1, 0, 2)[None]

    return (y_prompt, y_sample, new_k_prompt, new_v_prompt, new_logf_prompt, new_pool_prompt,
            new_k_sample, new_v_sample, new_logf_sample, new_pool_sample)
```

```python
import functools

import jax
import jax.numpy as jnp
from jax import lax
from jax.experimental import pallas as pl
from jax.experimental.pallas import tpu as pltpu

F32 = jnp.float32
BF16 = jnp.bfloat16

RMS_EPS = 1e-6
POOL_WINDOWS = (2, 4, 8, 16)
N_MODS = 9
LANES = 128
VMEM_LIMIT_BYTES = 56 * 1024 * 1024

ROW_TILE = 512
FFN_CHUNK = 256
ATTN_TILE = 512
DECODE_PAGES_PER_STEP = 16

_NT_DIMS = (((1,), (1,)), ((), ()))


def _dot(a, b):
    return jnp.dot(a, b, preferred_element_type=F32)


def _dot_nt(a, b):
    return lax.dot_general(a, b, _NT_DIMS, preferred_element_type=F32)


def _split3(x):
    hi = x.astype(BF16)
    r = x - hi.astype(F32)
    mid = r.astype(BF16)
    lo = (r - mid.astype(F32)).astype(BF16)
    return hi, mid, lo


def _split3_f32(x):
    hi, mid, lo = _split3(x)
    return hi.astype(F32), mid.astype(F32), lo.astype(F32)


def _dot_exact_left(x, ones_mat):
    hi, mid, lo = _split3(x)
    return _dot(hi, ones_mat) + _dot(mid, ones_mat) + _dot(lo, ones_mat)


def _dot_exact_right(ones_mat, x):
    hi, mid, lo = _split3(x)
    return _dot(ones_mat, hi) + _dot(ones_mat, mid) + _dot(ones_mat, lo)


def _silu(x):
    return x * jax.nn.sigmoid(x)


def _log_sigmoid(x):
    return jnp.minimum(x, 0.0) - jnp.log1p(jnp.exp(-jnp.abs(x)))


def _rmsnorm(x, g):
    return x * lax.rsqrt(jnp.mean(x * x, axis=-1, keepdims=True) + RMS_EPS) * g


def _rows(m, n):
    if m.shape[0] in (1, n):
        return m
    return jnp.concatenate([m] * (n // m.shape[0]), axis=0)


def _modnorm(x, g, shift, scale):
    n = x.shape[0]
    return _rmsnorm(x, g) * (1.0 + _rows(scale, n)) + _rows(shift, n)


def _resident(shape):
    zeros = (0,) * len(shape)
    return pl.BlockSpec(shape, lambda *_: zeros, pipeline_mode=pl.Buffered(1))


def _params():
    return pltpu.CompilerParams(vmem_limit_bytes=VMEM_LIMIT_BYTES)


def _ada_kernel(c_ref, w_ref, b_ref, o_ref):
    s = _silu(c_ref[...]).astype(BF16)
    o_ref[...] = _dot(s, w_ref[...].astype(BF16)) + b_ref[...]


def _ada(c, w, b):
    n, d = c.shape
    m = w.shape[1]
    tn = d
    return pl.pallas_call(
        _ada_kernel,
        grid=(m // tn,),
        in_specs=[
            pl.BlockSpec((n, d), lambda j: (0, 0)),
            pl.BlockSpec((d, tn), lambda j: (0, j)),
            pl.BlockSpec((1, tn), lambda j: (0, j)),
        ],
        out_specs=pl.BlockSpec((n, tn), lambda j: (0, j)),
        out_shape=jax.ShapeDtypeStruct((n, m), F32),
        compiler_params=_params(),
        name="ada_params",
    )(c, w, b)


def _ffn_begin(x_ref, g_ref, sh_ref, sc_ref, h_ref, acc_ref):
    h_ref[...] = _modnorm(x_ref[...], g_ref[...], sh_ref[...], sc_ref[...]).astype(BF16)
    acc_ref[...] = jnp.zeros_like(acc_ref)


def _ffn_chunk(c, h_ref, acc_ref, w1_ref, w3_ref, w2_ref):
    off = pl.multiple_of(c * FFN_CHUNK, FFN_CHUNK)
    h = h_ref[...]
    a = _dot(h, w1_ref[:, pl.ds(off, FFN_CHUNK)])
    b = _dot(h, w3_ref[:, pl.ds(off, FFN_CHUNK)])
    act = (_silu(a) * b).astype(BF16)
    acc_ref[...] += _dot(act, w2_ref[pl.ds(off, FFN_CHUNK), :])


def _ffn_end(x_ref, gt_ref, acc_ref):
    x = x_ref[...]
    return x + 0.5 * _rows(gt_ref[...], x.shape[0]) * acc_ref[...]


def _ffn_kernel(x_ref, g_ref, sh_ref, sc_ref, gt_ref, w1_ref, w3_ref, w2_ref, *rest, final_norm):
    gf_ref = rest[0] if final_norm else None
    o_ref, h_ref, acc_ref = rest[int(final_norm):]
    _ffn_begin(x_ref, g_ref, sh_ref, sc_ref, h_ref, acc_ref)

    def chunk(c, carry):
        _ffn_chunk(c, h_ref, acc_ref, w1_ref, w3_ref, w2_ref)
        return carry

    lax.fori_loop(0, w1_ref.shape[1] // FFN_CHUNK, chunk, 0, unroll=True)
    out = _ffn_end(x_ref, gt_ref, acc_ref)
    if final_norm:
        out = _rmsnorm(out, gf_ref[...])
    o_ref[...] = out


def _ffn(x, g, mods, mod_spec, k0, w1, w3, w2, gf=None, first_tile=0, in_place=False):
    r, d = x.shape
    dff = w1.shape[1]
    tm = ROW_TILE
    in_specs = [
        pl.BlockSpec((tm, d), lambda i: (i + first_tile, 0)),
        _resident((1, d)),
        mod_spec(k0),
        mod_spec(k0 + 1),
        mod_spec(k0 + 2),
        _resident((d, dff)),
        _resident((d, dff)),
        _resident((dff, d)),
    ]
    args = [x, g, mods, mods, mods, w1, w3, w2]
    if gf is not None:
        in_specs.append(_resident((1, d)))
        args.append(gf)
    aliases = {0: 0} if in_place else {}
    return pl.pallas_call(
        functools.partial(_ffn_kernel, final_norm=gf is not None),
        grid=(r // tm - first_tile,),
        in_specs=in_specs,
        out_specs=pl.BlockSpec((tm, d), lambda i: (i + first_tile, 0)),
        out_shape=jax.ShapeDtypeStruct((r, d), F32),
        input_output_aliases=aliases,
        scratch_shapes=[pltpu.VMEM((tm, d), BF16), pltpu.VMEM((tm, d), F32)],
        compiler_params=_params(),
        name="ffn_final" if gf is not None else "ffn",
    )(*args)


N_BIAS_PARTS = 3
LOG2_E = 1.4426950408889634
PV_SUM_ROWS = 16


def _inproj_prompt_kernel(x_ref, g_ref, sh_ref, sc_ref, wqT_ref, wkT_ref, wvT_ref, wf_ref, wfT_ref,
                          bf_ref, bfT_ref, wu_ref,
                          qTa_ref, ka_ref, vTa_ref, kT_ref, vT_ref, lfT_ref, u_ref,
                          carry_ref, carryT_ref, *, q_scale, head_dim):
    @pl.when(pl.program_id(1) == 0)
    def _():
        carry_ref[...] = jnp.zeros_like(carry_ref)
        carryT_ref[...] = jnp.zeros_like(carryT_ref)

    tm = x_ref.shape[0]
    page = kT_ref.shape[-1]
    n_heads, depth, _ = qTa_ref.shape
    h = _modnorm(x_ref[...], g_ref[...], sh_ref[...], sc_ref[...]).astype(BF16)
    u_ref[...] = _dot(h, wu_ref[...])
    kT = _dot_nt(wkT_ref[...], h)
    vT = _dot_nt(wvT_ref[...], h)
    lf = _log_sigmoid(_dot(h, wf_ref[...]) + bf_ref[...])
    lfT = _log_sigmoid(_dot_nt(wfT_ref[...], h) + bfT_ref[...])
    for j in range(tm // page):
        cols = slice(j * page, (j + 1) * page)
        kT_ref[j] = kT[:, cols]
        vT_ref[j] = vT[:, cols]
        lfT_ref[j] = lfT[:, cols]
    r = lax.broadcasted_iota(jnp.int32, (tm, tm), 0)
    c = lax.broadcasted_iota(jnp.int32, (tm, tm), 1)
    cf = _dot_exact_right((c <= r).astype(BF16), lf) + carry_ref[...]
    carry_ref[...] = cf[tm - 1:tm, :]
    cfT = _dot_exact_left(lfT, (r <= c).astype(BF16)) + carryT_ref[:, :1]
    carryT_ref[...] = jnp.broadcast_to(cfT[:, tm - 1:tm], carryT_ref.shape)
    qT = _dot_nt(wqT_ref[...], h) * (q_scale * LOG2_E)
    k = kT.T
    sub = lax.broadcasted_iota(jnp.int32, (8, tm), 0)
    lane = lax.broadcasted_iota(jnp.int32, (tm, depth), 1)
    pad_rows = jnp.zeros((depth - head_dim - 8, tm), F32)
    sum_rows = vTa_ref.shape[1] - head_dim
    ones_row = jnp.where(lax.broadcasted_iota(jnp.int32, (sum_rows, tm), 0) == 0, 1.0, 0.0)
    for hd in range(n_heads):
        rows = slice(hd * head_dim, (hd + 1) * head_dim)
        vTa_ref[hd] = jnp.concatenate([vT[rows, :], ones_row], axis=0).astype(BF16)
        f_hi, f_mid, f_lo = _split3_f32(jnp.broadcast_to(cfT[hd:hd + 1, :] * LOG2_E, (8, tm)))
        q_bias = jnp.where(sub == 0, f_hi, jnp.where(sub == 1, f_mid, jnp.where(sub == 2, f_lo,
                 jnp.where(sub < 2 * N_BIAS_PARTS, 1.0, 0.0))))
        qTa_ref[hd] = jnp.concatenate([qT[rows, :], q_bias, pad_rows], axis=0).astype(BF16)
        n_hi, n_mid, n_lo = _split3_f32(jnp.broadcast_to(cf[:, hd:hd + 1] * -LOG2_E, (tm, depth)))
        k_bias = jnp.where(lane == head_dim + 3, n_hi, jnp.where(lane == head_dim + 4, n_mid,
                 jnp.where(lane == head_dim + 5, n_lo, jnp.where(lane < head_dim + N_BIAS_PARTS, 1.0, 0.0))))
        blk = k[:, (hd // 2) * depth:(hd // 2 + 1) * depth]
        if hd % 2:
            blk = pltpu.roll(blk, head_dim, 1)
        ka_ref[hd] = jnp.where(lane < head_dim, blk, k_bias).astype(BF16)


def _inproj_prompt(x, g, mods, k0, wqT, wkT, wvT, wf, wfT, bf, bfT, wu, page, n_heads, head_dim, q_scale):
    b, s, d = x.shape
    a = wkT.shape[0]
    pw = wu.shape[1]
    tm = ROW_TILE
    npg = tm // page
    depth = 2 * head_dim
    assert depth == LANES and depth - head_dim >= 8

    def mod_spec(k):
        return pl.BlockSpec((None, 1, d), lambda bi, ti: (bi, 0, k))

    out_shape = (
        jax.ShapeDtypeStruct((b, n_heads, depth, s), BF16),
        jax.ShapeDtypeStruct((b, n_heads, s, depth), BF16),
        jax.ShapeDtypeStruct((b, n_heads, head_dim + PV_SUM_ROWS, s), BF16),
        jax.ShapeDtypeStruct((b, s // page, a, page), F32),
        jax.ShapeDtypeStruct((b, s // page, a, page), F32),
        jax.ShapeDtypeStruct((b, s // page, n_heads, page), F32),
        jax.ShapeDtypeStruct((b, s, pw), F32),
    )
    out_specs = (
        pl.BlockSpec((None, n_heads, depth, tm), lambda bi, ti: (bi, 0, 0, ti)),
        pl.BlockSpec((None, n_heads, tm, depth), lambda bi, ti: (bi, 0, ti, 0)),
        pl.BlockSpec((None, n_heads, head_dim + PV_SUM_ROWS, tm), lambda bi, ti: (bi, 0, 0, ti)),
        pl.BlockSpec((None, npg, a, page), lambda bi, ti: (bi, ti, 0, 0)),
        pl.BlockSpec((None, npg, a, page), lambda bi, ti: (bi, ti, 0, 0)),
        pl.BlockSpec((None, npg, n_heads, page), lambda bi, ti: (bi, ti, 0, 0)),
        pl.BlockSpec((None, tm, pw), lambda bi, ti: (bi, ti, 0)),
    )
    weights = (wqT, wkT, wvT, wf, wfT, bf, bfT, wu)
    in_specs = [
        pl.BlockSpec((None, tm, d), lambda bi, ti: (bi, ti, 0)),
        _resident((1, d)),
        mod_spec(k0),
        mod_spec(k0 + 1),
    ] + [_resident(w.shape) for w in weights]
    return pl.pallas_call(
        functools.partial(_inproj_prompt_kernel, q_scale=q_scale, head_dim=head_dim),
        grid=(b, s // tm),
        in_specs=in_specs,
        out_specs=out_specs,
        out_shape=out_shape,
        scratch_shapes=[pltpu.VMEM((1, LANES), F32), pltpu.VMEM((n_heads, LANES), F32)],
        compiler_params=_params(),
        name="inproj_prompt",
    )(x, g, mods, mods, *weights)


def _attn_prompt_kernel(qTa_ref, ka_ref, vTa_ref, o_ref, m_ref, acc_ref, s_ref, *, head_dim):
    qi = pl.program_id(2)
    tq = qTa_ref.shape[-1]
    tk = tq
    m_ref[...] = jnp.full_like(m_ref, -jnp.inf)
    acc_ref[...] = jnp.zeros_like(acc_ref)

    def scores_into(slot, j):
        off = pl.multiple_of(j * tk, tk)
        for i in range(2):
            s_ref[slot, i] = _dot(ka_ref[i, pl.ds(off, tk), :], qTa_ref[i])

    def consume(slot, j, masked):
        off = pl.multiple_of(j * tk, tk)
        for i in range(2):
            s = s_ref[slot, i]
            if masked:
                key = lax.broadcasted_iota(jnp.int32, (tk, tq), 0)
                qry = lax.broadcasted_iota(jnp.int32, (tk, tq), 1)
                s = jnp.where(key <= qry, s, -jnp.inf)
            m_prev = m_ref[i]
            m_new = jnp.maximum(m_prev, jnp.max(s, axis=0, keepdims=True))
            alpha = jnp.exp2(m_prev - m_new)
            p = jnp.exp2(s - m_new)
            acc_ref[i] = alpha * acc_ref[i] + _dot(vTa_ref[i, :, pl.ds(off, tk)], p.astype(BF16))
            m_ref[i] = m_new

    def two_full_blocks(jj, carry):
        scores_into(1, 2 * jj + 1)
        consume(0, 2 * jj, False)
        scores_into(0, 2 * jj + 2)
        consume(1, 2 * jj + 1, False)
        return carry

    scores_into(0, 0)
    lax.fori_loop(0, qi // 2, two_full_blocks, 0)

    @pl.when(qi % 2 == 0)
    def _():
        consume(0, qi, True)

    @pl.when(qi % 2 == 1)
    def _():
        scores_into(1, qi)
        consume(0, qi - 1, False)
        consume(1, qi, True)

    oT = jnp.concatenate(
        [acc_ref[i, :head_dim, :] / acc_ref[i, head_dim:head_dim + 1, :] for i in range(2)], axis=0)
    o_ref[...] = oT.T.astype(BF16)


def _attn_prompt(qTa, ka, vTa, head_dim):
    b, n_heads, depth, s = qTa.shape
    v_rows = vTa.shape[2]
    a = n_heads * head_dim
    tq = ATTN_TILE
    return pl.pallas_call(
        functools.partial(_attn_prompt_kernel, head_dim=head_dim),
        grid=(b, n_heads // 2, s // tq),
        in_specs=[
            pl.BlockSpec((None, 2, depth, tq), lambda bi, hp, qi: (bi, hp, 0, qi)),
            pl.BlockSpec((None, 2, s, depth), lambda bi, hp, qi: (bi, hp, 0, 0)),
            pl.BlockSpec((None, 2, v_rows, s), lambda bi, hp, qi: (bi, hp, 0, 0)),
        ],
        out_specs=pl.BlockSpec((None, tq, depth), lambda bi, hp, qi: (bi, qi, hp)),
        out_shape=jax.ShapeDtypeStruct((b, s, a), BF16),
        scratch_shapes=[
            pltpu.VMEM((2, 1, tq), F32),
            pltpu.VMEM((2, v_rows, tq), F32),
            pltpu.VMEM((2, 2, tq, tq), F32),
        ],
        compiler_params=_params(),
        name="attn_prompt",
    )(qTa, ka, vTa)


def _pool_prompt_kernel(u_ref, wp_ref, ps_ref, o_ref, halo_ref):
    ti = pl.program_id(1)

    @pl.when(ti == 0)
    def _():
        halo_ref[...] = jnp.zeros_like(halo_ref)

    tm = u_ref.shape[0]
    nh = halo_ref.shape[0]
    gd = wp_ref.shape[1]
    u = u_ref[...]
    ext = jnp.concatenate([halo_ref[...], u], axis=0)
    halo_ref[...] = u[tm - nh:, :]
    sums = {1: ext}
    w = 1
    while w < max(POOL_WINDOWS):
        sums[2 * w] = sums[w] + pltpu.roll(sums[w], w, 0)
        w *= 2
    pos = ti * tm + lax.broadcasted_iota(jnp.int32, (tm, 1), 0)
    ys = []
    for gi, w in enumerate(POOL_WINDOWS):
        lanes = slice(gi * gd, (gi + 1) * gd)
        count = jnp.minimum(pos + 1, w).astype(F32)
        d = sums[w][nh:, lanes] / count - u[:, lanes]
        ys.append(_dot(d.astype(BF16), wp_ref[gi]))
    o_ref[...] = (jnp.concatenate(ys, axis=1) * ps_ref[...]).astype(BF16)


def _pool_prompt(u, wp, ps):
    b, s, pw = u.shape
    tm = ROW_TILE
    halo = 16
    assert halo >= max(POOL_WINDOWS)
    return pl.pallas_call(
        _pool_prompt_kernel,
        grid=(b, s // tm),
        in_specs=[
            pl.BlockSpec((None, tm, pw), lambda bi, ti: (bi, ti, 0)),
            _resident(wp.shape),
            _resident(ps.shape),
        ],
        out_specs=pl.BlockSpec((None, tm, pw), lambda bi, ti: (bi, ti, 0)),
        out_shape=jax.ShapeDtypeStruct((b, s, pw), BF16),
        scratch_shapes=[pltpu.VMEM((halo, pw), F32)],
        compiler_params=_params(),
        name="pool_prompt",
    )(u, wp, ps)


def _outproj_kernel(x_ref, a_ref, p_ref, gt_ref, wa_ref, wp_ref, o_ref):
    x = x_ref[...]
    mix = _dot(a_ref[...], wa_ref[...]) + _dot(p_ref[...], wp_ref[...])
    o_ref[...] = x + _rows(gt_ref[...], x.shape[0]) * mix


def _outproj(x, attn, pool, mods, mod_spec, k, wa, wp):
    r, d = x.shape
    tm = ROW_TILE
    return pl.pallas_call(
        _outproj_kernel,
        grid=(r // tm,),
        in_specs=[
            pl.BlockSpec((tm, d), lambda i: (i, 0)),
            pl.BlockSpec((tm, attn.shape[1]), lambda i: (i, 0)),
            pl.BlockSpec((tm, pool.shape[1]), lambda i: (i, 0)),
            mod_spec(k),
            _resident(wa.shape),
            _resident(wp.shape),
        ],
        out_specs=pl.BlockSpec((tm, d), lambda i: (i, 0)),
        out_shape=jax.ShapeDtypeStruct((r, d), F32),
        compiler_params=_params(),
        name="outproj",
    )(x, attn, pool, mods, wa, wp)


def _inproj_sample_kernel(x_ref, g_ref, sh_ref, sc_ref, wqkv_ref, wkT_ref, wvT_ref, wf_ref, wfT_ref,
                          bf_ref, bfT_ref, wu_ref,
                          q_ref, k_ref, v_ref, kT_ref, vT_ref, lfT_ref, c_ref, u_ref, *, q_scale):
    n_t, _, nb = kT_ref.shape
    n_heads = c_ref.shape[-1]
    h = _modnorm(x_ref[...], g_ref[...], sh_ref[...], sc_ref[...]).astype(BF16)
    z = _dot(h, wqkv_ref[...])
    a = z.shape[1] // 3
    q_ref[...] = (z[:, :a] * q_scale).astype(BF16)
    k_ref[...] = z[:, a:2 * a]
    v_ref[...] = z[:, 2 * a:]
    u_ref[...] = _dot(h, wu_ref[...])
    kT = _dot_nt(wkT_ref[...], h)
    vT = _dot_nt(wvT_ref[...], h)
    lf = _log_sigmoid(_dot(h, wf_ref[...]) + bf_ref[...])
    lfT = _log_sigmoid(_dot_nt(wfT_ref[...], h) + bfT_ref[...])
    run = None
    parts = []
    for t in range(n_t):
        cols = slice(t * nb, (t + 1) * nb)
        kT_ref[t] = kT[:, cols]
        vT_ref[t] = vT[:, cols]
        lfT_ref[t] = lfT[:, cols]
        blk = lf[t * nb:(t + 1) * nb, :]
        run = blk if run is None else run + blk
        parts.append(run)
    c_ref[...] = jnp.concatenate(parts, axis=0)[:, :n_heads]


def _inproj_sample(x, g, mods, mod_spec, k0, wqkv, wkT, wvT, wf, wfT, bf, bfT, wu, n_t, n_heads, q_scale):
    r, d = x.shape
    a = wkT.shape[0]
    pw = wu.shape[1]
    nb = r // n_t
    out_shape = (
        jax.ShapeDtypeStruct((r, a), BF16),
        jax.ShapeDtypeStruct((r, a), F32),
        jax.ShapeDtypeStruct((r, a), F32),
        jax.ShapeDtypeStruct((n_t, a, nb), F32),
        jax.ShapeDtypeStruct((n_t, a, nb), F32),
        jax.ShapeDtypeStruct((n_t, n_heads, nb), F32),
        jax.ShapeDtypeStruct((r, n_heads), F32),
        jax.ShapeDtypeStruct((r, pw), F32),
    )
    whole = lambda shp: pl.BlockSpec(shp, lambda i: (0,) * len(shp))
    in_specs = [whole((r, d)), whole((1, d)), mod_spec(k0), mod_spec(k0 + 1)] + [
        whole(w.shape) for w in (wqkv, wkT, wvT, wf, wfT, bf, bfT, wu)
    ]
    return pl.pallas_call(
        functools.partial(_inproj_sample_kernel, q_scale=q_scale),
        grid=(1,),
        in_specs=in_specs,
        out_specs=tuple(whole(o.shape) for o in out_shape),
        out_shape=out_shape,
        compiler_params=_params(),
        name="inproj_sample",
    )(x, g, mods, mods, wqkv, wkT, wvT, wf, wfT, bf, bfT, wu)


def _decode_kernel(pt_ref, q_ref, cq_ref, c2_ref, kn_ref, vn_ref, kc_ref, vc_ref, lfc_ref,
                   x_ref, g_ref, sh_ref, sc_ref, gt_ref, w1_ref, w3_ref, w2_ref, *rest,
                   head_dim, ffn_tiles, final_norm):
    gf_ref = rest[0] if final_norm else None
    o_ref, y_ref, kbuf, vbuf, lfbuf, sems, m_ref, l_ref, acc_ref, r_ref, h_ref, facc_ref = rest[int(final_norm):]
    g_n = kbuf.shape[1]
    seq = pl.program_id(0)
    step = pl.program_id(1)
    n_steps = pl.num_programs(1)
    n_pages = n_steps * g_n
    lin = seq * n_steps + step
    slot = lin % 2

    def page_copies(seq_i, step_i, slot_i):
        copies = []
        for i in range(g_n):
            pid = pt_ref[seq_i, n_pages - 1 - (step_i * g_n + i)]
            copies.append(pltpu.make_async_copy(kc_ref.at[pid], kbuf.at[slot_i, i], sems.at[slot_i, 0, i]))
            copies.append(pltpu.make_async_copy(vc_ref.at[pid], vbuf.at[slot_i, i], sems.at[slot_i, 1, i]))
            copies.append(pltpu.make_async_copy(lfc_ref.at[pid], lfbuf.at[slot_i, i], sems.at[slot_i, 2, i]))
        return copies

    @pl.when(lin == 0)
    def _():
        for cp in page_copies(0, 0, 0):
            cp.start()

    @pl.when(lin + 1 < pl.num_programs(0) * n_steps)
    def _():
        wrap = step + 1 == n_steps
        for cp in page_copies(jnp.where(wrap, seq + 1, seq), jnp.where(wrap, 0, step + 1), 1 - slot):
            cp.start()

    n_chunks = w1_ref.shape[1] // FFN_CHUNK
    chunk = lin % n_chunks

    @pl.when(lin < ffn_tiles * n_chunks)
    def _():
        @pl.when(chunk == 0)
        def _():
            _ffn_begin(x_ref, g_ref, sh_ref, sc_ref, h_ref, facc_ref)

        _ffn_chunk(chunk, h_ref, facc_ref, w1_ref, w3_ref, w2_ref)

        @pl.when(chunk == n_chunks - 1)
        def _():
            y = _ffn_end(x_ref, gt_ref, facc_ref)
            y_ref[...] = _rmsnorm(y, gf_ref[...]) if final_norm else y

    for cp in page_copies(seq, step, slot):
        cp.wait()
    k_refs = [kbuf.at[slot, i] for i in range(g_n)]
    v_refs = [vbuf.at[slot, i] for i in range(g_n)]
    lf_refs = [lfbuf.at[slot, i] for i in range(g_n)]
    n_rows, width = q_ref.shape
    n_heads = width // head_dim
    n_t = n_rows // n_heads
    row = lax.broadcasted_iota(jnp.int32, (n_rows, 1), 0)
    lane = lax.broadcasted_iota(jnp.int32, (1, width), 1)
    own = (lane // head_dim) == (row % n_heads)
    t_row = row // n_heads
    q = q_ref[...]
    qbd = jnp.where(own, q, jnp.zeros_like(q))
    cq = cq_ref[...]

    @pl.when(step == 0)
    def _():
        qf = qbd.astype(F32)
        kn = kn_ref[...]
        vn = vn_ref[...]
        c2 = c2_ref[...]
        sc = []
        for s in range(n_t):
            col = jnp.sum(qf * kn[s:s + 1, :], axis=1, keepdims=True) + cq - c2[:, s:s + 1]
            sc.append(jnp.where(t_row >= s, col, -jnp.inf))
        m0 = functools.reduce(jnp.maximum, sc)
        l0 = jnp.zeros_like(m0)
        a0 = jnp.zeros(acc_ref.shape, F32)
        for s in range(n_t):
            p = jnp.exp(sc[s] - m0)
            l0 = l0 + p
            a0 = a0 + p * vn[s:s + 1, :]
        m_ref[...] = m0
        l_ref[...] = l0
        acc_ref[...] = a0
        r_ref[...] = jnp.zeros_like(r_ref)

    page = lf_refs[0].shape[-1]
    pj = lax.broadcasted_iota(jnp.int32, (page, page), 0)
    ps = lax.broadcasted_iota(jnp.int32, (page, page), 1)
    later = (pj > ps).astype(BF16)
    r_run = r_ref[...]
    lf_all = jnp.concatenate([lf_refs[i][...] for i in range(g_n)], axis=0)
    suffix_all = _dot_exact_left(lf_all, later)
    totals = jnp.sum(lf_all, axis=1, keepdims=True)
    scores = []
    for i in range(g_n):
        rows = slice(i * n_heads, (i + 1) * n_heads)
        suffix = suffix_all[rows] + r_run
        r_run = r_run + totals[rows]
        bias = jnp.concatenate([suffix] * n_t, axis=0) + cq
        scores.append(_dot(qbd, k_refs[i][...].astype(BF16)) + bias)
    r_ref[...] = r_run
    m_prev = m_ref[...]
    m_new = functools.reduce(jnp.maximum, [m_prev] + [jnp.max(s, axis=1, keepdims=True) for s in scores])
    alpha = jnp.exp(m_prev - m_new)
    l_new = alpha * l_ref[...]
    acc = alpha * acc_ref[...]
    for i in range(g_n):
        p = jnp.exp(scores[i] - m_new)
        l_new = l_new + jnp.sum(p, axis=1, keepdims=True)
        acc = acc + _dot_nt(p.astype(BF16), v_refs[i][...].astype(BF16))
    m_ref[...] = m_new
    l_ref[...] = l_new
    acc_ref[...] = acc

    @pl.when(step == pl.num_programs(1) - 1)
    def _():
        o = jnp.where(own, acc, 0.0) / l_new
        o_ref[...] = o.reshape(n_t, n_heads, width).sum(axis=1)


def _decode_and_ffn(page_table, q_rep, cq, c2, k_new, v_new, cache_kT, cache_vT, cache_lfT, head_dim,
                    x, g, mods, k0, tiles_per_seq, w1, w3, w2, gf=None, ffn_tiles=None, in_place=False):
    bd, n_pages = page_table.shape
    _, n_rows, width = q_rep.shape
    n_t = k_new.shape[1]
    page = cache_kT.shape[-1]
    n_heads = cache_lfT.shape[1]
    g_n = DECODE_PAGES_PER_STEP
    n_steps = n_pages // g_n
    r, d = x.shape
    dff = w1.shape[1]
    tm = ROW_TILE
    ffn_tiles = r // tm if ffn_tiles is None else ffn_tiles
    n_chunks = dff // FFN_CHUNK
    assert ffn_tiles * n_chunks <= bd * n_steps, "not enough decode steps to carry the SwiGLU chunks"

    def tile(b, s):
        return jnp.minimum((b * n_steps + s) // n_chunks, ffn_tiles - 1)

    def mod_spec(k):
        return pl.BlockSpec((None, 1, d), lambda b, s, pt: (tile(b, s) // tiles_per_seq, 0, k))

    per_seq = lambda shp: pl.BlockSpec((None,) + shp, lambda b, s, pt: (b, 0, 0))
    in_hbm = pl.BlockSpec(memory_space=pl.ANY)
    row_tile = pl.BlockSpec((tm, d), lambda b, s, pt: (tile(b, s), 0))
    in_specs = [
        per_seq((n_rows, width)),
        per_seq((n_rows, 1)),
        per_seq((n_rows, n_t)),
        per_seq((n_t, width)),
        per_seq((n_t, width)),
        in_hbm,
        in_hbm,
        in_hbm,
        row_tile,
        _resident((1, d)),
        mod_spec(k0),
        mod_spec(k0 + 1),
        mod_spec(k0 + 2),
        _resident((d, dff)),
        _resident((d, dff)),
        _resident((dff, d)),
    ]
    args = [page_table, q_rep, cq, c2, k_new, v_new, cache_kT, cache_vT, cache_lfT]
    x_operand = len(args)
    args += [x, g, mods, mods, mods, w1, w3, w2]
    if gf is not None:
        in_specs.append(_resident((1, d)))
        args.append(gf)
    grid_spec = pltpu.PrefetchScalarGridSpec(
        num_scalar_prefetch=1,
        grid=(bd, n_steps),
        in_specs=in_specs,
        out_specs=(per_seq((n_t, width)), row_tile),
        scratch_shapes=[
            pltpu.VMEM((2, g_n, width, page), F32),
            pltpu.VMEM((2, g_n, width, page), F32),
            pltpu.VMEM((2, g_n, n_heads, page), F32),
            pltpu.SemaphoreType.DMA((2, 3, g_n)),
            pltpu.VMEM((n_rows, 1), F32),
            pltpu.VMEM((n_rows, 1), F32),
            pltpu.VMEM((n_rows, width), F32),
            pltpu.VMEM((n_heads, LANES), F32),
            pltpu.VMEM((tm, d), BF16),
            pltpu.VMEM((tm, d), F32),
        ],
    )
    return pl.pallas_call(
        functools.partial(_decode_kernel, head_dim=head_dim, ffn_tiles=ffn_tiles, final_norm=gf is not None),
        grid_spec=grid_spec,
        out_shape=(jax.ShapeDtypeStruct((bd, n_t, width), F32), jax.ShapeDtypeStruct((r, d), F32)),
        input_output_aliases={x_operand: 1} if in_place else {},
        compiler_params=_params(),
        name="decode_attn_ffn",
    )(*args)


def _pool_sample_kernel(st_ref, u_ref, wp_ref, ps_ref, o_ref):
    n_state = st_ref.shape[0]
    n_t = u_ref.shape[0]
    gd = wp_ref.shape[1]

    def rows(idx, lanes):
        return st_ref[idx, :, lanes] if idx < n_state else u_ref[idx - n_state, :, lanes]

    for t in range(n_t):
        ys = []
        for gi, w in enumerate(POOL_WINDOWS):
            lanes = slice(gi * gd, (gi + 1) * gd)
            cur = rows(n_state + t, lanes)
            total = cur
            for j in range(1, w):
                total = total + rows(n_state + t - j, lanes)
            d = total / float(w) - cur
            ys.append(_dot(d.astype(BF16), wp_ref[gi]))
        o_ref[t] = (jnp.concatenate(ys, axis=1) * ps_ref[...]).astype(BF16)


def _pool_sample(state_tm, u_tm, wp, ps):
    assert state_tm.shape[0] >= max(POOL_WINDOWS) - 1
    whole = lambda shp: pl.BlockSpec(shp, lambda i: (0,) * len(shp))
    return pl.pallas_call(
        _pool_sample_kernel,
        grid=(1,),
        in_specs=[whole(state_tm.shape), whole(u_tm.shape), whole(wp.shape), whole(ps.shape)],
        out_specs=whole(u_tm.shape),
        out_shape=jax.ShapeDtypeStruct(u_tm.shape, BF16),
        compiler_params=_params(),
        name="pool_sample",
    )(state_tm, u_tm, wp, ps)


def kernel(x_prompt, x_sample, cache_k, cache_v, cache_logf, state_pool, page_table, c_prompt, c_sample, norm_ffn1, norm_mix, norm_ffn2, norm_final, w_ada, b_ada, ffn1_w1, ffn1_w3, ffn1_w2, ffn2_w1, ffn2_w3, ffn2_w2, w_in, b_f, w_o, w_pool, pool_scale):
    bp, s_len, d = x_prompt.shape
    bd, t_len, _ = x_sample.shape
    depth, n_phys, page, n_heads, head_dim = cache_k.shape
    assert depth == 1, "single-layer model"
    a = n_heads * head_dim
    pw = state_pool.shape[-1]
    n_state = state_pool.shape[2]
    q_scale = head_dim ** -0.5
    li = 0

    n_pad = (-(bd + bp)) % 16
    c_all = jnp.concatenate([c_sample, c_prompt, jnp.zeros((n_pad, d), F32)], axis=0)
    mods = _ada(c_all, w_ada[li], b_ada[li][None, :])
    mods_p = mods[bd:bd + bp].reshape(bp, 1, N_MODS * d)
    tiles_per_seq = s_len // ROW_TILE

    def prompt_mod_from(first_tile):
        return lambda k: pl.BlockSpec((None, 1, d), lambda i: ((i + first_tile) // tiles_per_seq, 0, k))

    prompt_mod = prompt_mod_from(0)

    def sample_mod(k):
        return pl.BlockSpec((bd, d), lambda i: (0, k))

    f1 = [w[li].astype(BF16) for w in (ffn1_w1, ffn1_w3, ffn1_w2)]
    f2 = [w[li].astype(BF16) for w in (ffn2_w1, ffn2_w3, ffn2_w2)]
    w_in_l = w_in[li]
    wqkv = w_in_l[:, :3 * a].astype(BF16)
    wqT = w_in_l[:, :a].T.astype(BF16)
    wkT = w_in_l[:, a:2 * a].T.astype(BF16)
    wvT = w_in_l[:, 2 * a:3 * a].T.astype(BF16)
    wf_cols = w_in_l[:, 3 * a:3 * a + n_heads]
    wf = jnp.pad(wf_cols, ((0, 0), (0, LANES - n_heads))).astype(BF16)
    wfT = wf_cols.T.astype(BF16)
    bf = jnp.pad(b_f[li][None, :], ((0, 0), (0, LANES - n_heads)))
    bfT = b_f[li][:, None]
    wu = w_in_l[:, 3 * a + n_heads:].astype(BF16)
    wo_a = w_o[li][:a].astype(BF16)
    wo_p = w_o[li][a:].astype(BF16)
    wp = w_pool[li].astype(BF16)
    ps = pool_scale[li][None, :]
    g1 = norm_ffn1[li][None, :]
    gm = norm_mix[li][None, :]
    g2 = norm_ffn2[li][None, :]
    gf = norm_final[None, :]

    n_rows = bd * t_len
    xs = x_sample.transpose(1, 0, 2).reshape(n_rows, d)
    xs = _ffn(xs, g1, mods, sample_mod, 0, *f1)
    q_s, k_s, v_s, kT_s, vT_s, lfT_s, c_s, u_s = _inproj_sample(
        xs, gm, mods, sample_mod, 3, wqkv, wkT, wvT, wf, wfT, bf, bfT, wu, t_len, n_heads, q_scale)

    def per_seq(z):
        return z.reshape(t_len, bd, z.shape[-1]).transpose(1, 0, 2)

    q_rep = jnp.repeat(per_seq(q_s), n_heads, axis=1)
    c_bt = per_seq(c_s)
    cq = c_bt.reshape(bd, t_len * n_heads, 1)
    c2 = jnp.broadcast_to(c_bt.transpose(0, 2, 1)[:, None], (bd, t_len, n_heads, t_len)).reshape(bd, t_len * n_heads, t_len)
    cache_kT = cache_k[li].transpose(0, 2, 3, 1).reshape(n_phys, a, page)
    cache_vT = cache_v[li].transpose(0, 2, 3, 1).reshape(n_phys, a, page)
    cache_lfT = cache_logf[li].transpose(0, 2, 1)

    k_new, v_new = per_seq(k_s), per_seq(v_s)
    n_tiles = bp * s_len // ROW_TILE
    n_steps = page_table.shape[1] // DECODE_PAGES_PER_STEP
    n_chunks = f1[0].shape[1] // FFN_CHUNK
    seq_a = min(bd, -(-n_tiles * n_chunks // n_steps))
    tiles_b = min(n_tiles, (bd - seq_a) * n_steps // n_chunks)
    if tiles_b == 0:
        seq_a = bd

    def decode_part(lo, hi, x, g, k0, w, **kw):
        sl = slice(lo, hi)
        return _decode_and_ffn(page_table[sl], q_rep[sl], cq[sl], c2[sl], k_new[sl], v_new[sl],
                               cache_kT, cache_vT, cache_lfT, head_dim, x, g, mods_p, k0, tiles_per_seq, *w, **kw)

    attn_s, xp = decode_part(0, seq_a, x_prompt.reshape(bp * s_len, d), g1, 0, f1)

    qTa_p, ka_p, vTa_p, kT_p, vT_p, lfT_p, u_p = _inproj_prompt(
        xp.reshape(bp, s_len, d), gm, mods_p, 3, wqT, wkT, wvT, wf, wfT, bf, bfT, wu,
        page, n_heads, head_dim, q_scale)
    attn_p = _attn_prompt(qTa_p, ka_p, vTa_p, head_dim)
    pool_p = _pool_prompt(u_p, wp, ps)
    xp = _outproj(xp, attn_p.reshape(bp * s_len, a), pool_p.reshape(bp * s_len, pw), mods_p, prompt_mod, 5, wo_a, wo_p)
    if tiles_b:
        attn_b, y_part = decode_part(seq_a, bd, xp, g2, 6, f2, gf=gf, ffn_tiles=tiles_b, in_place=True)
        attn_s = jnp.concatenate([attn_s, attn_b], axis=0)
        y_prompt = _ffn(y_part, g2, mods_p, prompt_mod_from(tiles_b), 6, *f2, gf=gf, first_tile=tiles_b, in_place=True)
    else:
        y_prompt = _ffn(xp, g2, mods_p, prompt_mod, 6, *f2, gf=gf)
    y_prompt = y_prompt.reshape(bp, s_len, d)

    n_pp = s_len // page
    new_k_prompt = kT_p.reshape(bp, n_pp, n_heads, head_dim, page).transpose(0, 1, 4, 2, 3)[None]
    new_v_prompt = vT_p.reshape(bp, n_pp, n_heads, head_dim, page).transpose(0, 1, 4, 2, 3)[None]
    new_logf_prompt = lfT_p.transpose(0, 1, 3, 2)[None]
    new_pool_prompt = u_p[:, s_len - n_state:][None]

    attn_s = attn_s.transpose(1, 0, 2).reshape(n_rows, a).astype(BF16)
    state_tm = state_pool[li].transpose(1, 0, 2)
    u_tm = u_s.reshape(t_len, bd, pw)
    pool_s = _pool_sample(state_tm, u_tm, wp, ps).reshape(n_rows, pw)
    xs = _outproj(xs, attn_s, pool_s, mods, sample_mod, 5, wo_a, wo_p)
    y_sample = _ffn(xs, g2, mods, sample_mod, 6, *f2, gf=gf).reshape(t_len, bd, d).transpose(1, 0, 2)

    new_k_sample = kT_s.reshape(t_len, n_heads, head_dim, bd).transpose(3, 0, 1, 2)[None]
    new_v_sample = vT_s.reshape(t_len, n_heads, head_dim, bd).transpose(3, 0, 1, 2)[None]
    new_logf_sample = lfT_s.transpose(2, 0, 1)[None]
    new_pool_sample = jnp.concatenate([state_tm[t_len:], u_tm], axis=0).transpose(1, 0, 2)[None]

    return (y_prompt, y_sample, new_k_prompt, new_v_prompt, new_logf_prompt, new_pool_prompt,
            new_k_sample, new_v_sample, new_logf_sample, new_pool_sample)
```

```python
import functools

import jax
import jax.numpy as jnp
from jax import lax
from jax.experimental import pallas as pl
from jax.experimental.pallas import tpu as pltpu

F32 = jnp.float32
BF16 = jnp.bfloat16

RMS_EPS = 1e-6
POOL_WINDOWS = (2, 4, 8, 16)
N_MODS = 9
LANES = 128
VMEM_LIMIT_BYTES = 56 * 1024 * 1024

ROW_TILE = 512
FFN_CHUNK = 256
ATTN_TILE = 512
DECODE_PAGES_PER_STEP = 16
COPY_PRIORITY = (0, 1, 0)

_NT_DIMS = (((1,), (1,)), ((), ()))


def _dot(a, b):
    return jnp.dot(a, b, preferred_element_type=F32)


def _dot_nt(a, b):
    return lax.dot_general(a, b, _NT_DIMS, preferred_element_type=F32)


def _split3(x):
    hi = x.astype(BF16)
    r = x - hi.astype(F32)
    mid = r.astype(BF16)
    lo = (r - mid.astype(F32)).astype(BF16)
    return hi, mid, lo


def _split3_f32(x):
    hi, mid, lo = _split3(x)
    return hi.astype(F32), mid.astype(F32), lo.astype(F32)


def _dot_exact_left(x, ones_mat):
    hi, mid, lo = _split3(x)
    return _dot(hi, ones_mat) + _dot(mid, ones_mat) + _dot(lo, ones_mat)


def _dot_exact_right(ones_mat, x):
    hi, mid, lo = _split3(x)
    return _dot(ones_mat, hi) + _dot(ones_mat, mid) + _dot(ones_mat, lo)


def _silu(x):
    return x * jax.nn.sigmoid(x)


def _log_sigmoid(x):
    return jnp.minimum(x, 0.0) - jnp.log1p(jnp.exp(-jnp.abs(x)))


def _rmsnorm(x, g):
    return x * lax.rsqrt(jnp.mean(x * x, axis=-1, keepdims=True) + RMS_EPS) * g


def _rows(m, n):
    if m.shape[0] in (1, n):
        return m
    return jnp.concatenate([m] * (n // m.shape[0]), axis=0)


def _modnorm(x, g, shift, scale):
    n = x.shape[0]
    return _rmsnorm(x, g) * (1.0 + _rows(scale, n)) + _rows(shift, n)


def _resident(shape):
    zeros = (0,) * len(shape)
    return pl.BlockSpec(shape, lambda *_: zeros, pipeline_mode=pl.Buffered(1))


def _params():
    return pltpu.CompilerParams(vmem_limit_bytes=VMEM_LIMIT_BYTES)


def _ada_kernel(c_ref, w_ref, b_ref, o_ref):
    s = _silu(c_ref[...]).astype(BF16)
    o_ref[...] = _dot(s, w_ref[...].astype(BF16)) + b_ref[...]


def _ada(c, w, b):
    n, d = c.shape
    m = w.shape[1]
    tn = d
    return pl.pallas_call(
        _ada_kernel,
        grid=(m // tn,),
        in_specs=[
            pl.BlockSpec((n, d), lambda j: (0, 0)),
            pl.BlockSpec((d, tn), lambda j: (0, j)),
            pl.BlockSpec((1, tn), lambda j: (0, j)),
        ],
        out_specs=pl.BlockSpec((n, tn), lambda j: (0, j)),
        out_shape=jax.ShapeDtypeStruct((n, m), F32),
        compiler_params=_params(),
        name="ada_params",
    )(c, w, b)


def _ffn_begin(x_ref, g_ref, sh_ref, sc_ref, h_ref, acc_ref):
    h_ref[...] = _modnorm(x_ref[...], g_ref[...], sh_ref[...], sc_ref[...]).astype(BF16)
    acc_ref[...] = jnp.zeros_like(acc_ref)


def _ffn_chunk(c, h_ref, acc_ref, w1_ref, w3_ref, w2_ref):
    off = pl.multiple_of(c * FFN_CHUNK, FFN_CHUNK)
    h = h_ref[...]
    a = _dot(h, w1_ref[:, pl.ds(off, FFN_CHUNK)])
    b = _dot(h, w3_ref[:, pl.ds(off, FFN_CHUNK)])
    act = (_silu(a) * b).astype(BF16)
    acc_ref[...] += _dot(act, w2_ref[pl.ds(off, FFN_CHUNK), :])


def _ffn_end(x_ref, gt_ref, acc_ref):
    x = x_ref[...]
    return x + 0.5 * _rows(gt_ref[...], x.shape[0]) * acc_ref[...]


def _ffn_kernel(x_ref, g_ref, sh_ref, sc_ref, gt_ref, w1_ref, w3_ref, w2_ref, *rest, final_norm):
    gf_ref = rest[0] if final_norm else None
    o_ref, h_ref, acc_ref = rest[int(final_norm):]
    _ffn_begin(x_ref, g_ref, sh_ref, sc_ref, h_ref, acc_ref)

    def chunk(c, carry):
        _ffn_chunk(c, h_ref, acc_ref, w1_ref, w3_ref, w2_ref)
        return carry

    lax.fori_loop(0, w1_ref.shape[1] // FFN_CHUNK, chunk, 0, unroll=True)
    out = _ffn_end(x_ref, gt_ref, acc_ref)
    if final_norm:
        out = _rmsnorm(out, gf_ref[...])
    o_ref[...] = out


def _ffn(x, g, mods, mod_spec, k0, w1, w3, w2, gf=None, first_tile=0, in_place=False):
    r, d = x.shape
    dff = w1.shape[1]
    tm = ROW_TILE
    in_specs = [
        pl.BlockSpec((tm, d), lambda i: (i + first_tile, 0)),
        _resident((1, d)),
        mod_spec(k0),
        mod_spec(k0 + 1),
        mod_spec(k0 + 2),
        _resident((d, dff)),
        _resident((d, dff)),
        _resident((dff, d)),
    ]
    args = [x, g, mods, mods, mods, w1, w3, w2]
    if gf is not None:
        in_specs.append(_resident((1, d)))
        args.append(gf)
    aliases = {0: 0} if in_place else {}
    return pl.pallas_call(
        functools.partial(_ffn_kernel, final_norm=gf is not None),
        grid=(r // tm - first_tile,),
        in_specs=in_specs,
        out_specs=pl.BlockSpec((tm, d), lambda i: (i + first_tile, 0)),
        out_shape=jax.ShapeDtypeStruct((r, d), F32),
        input_output_aliases=aliases,
        scratch_shapes=[pltpu.VMEM((tm, d), BF16), pltpu.VMEM((tm, d), F32)],
        compiler_params=_params(),
        name="ffn_final" if gf is not None else "ffn",
    )(*args)


N_BIAS_PARTS = 3
LOG2_E = 1.4426950408889634
PV_SUM_ROWS = 16


def _inproj_prompt_kernel(x_ref, g_ref, sh_ref, sc_ref, wqT_ref, wkT_ref, wvT_ref, wf_ref, wfT_ref,
                          bf_ref, bfT_ref, wu_ref,
                          qTa_ref, ka_ref, vTa_ref, kT_ref, vT_ref, lfT_ref, u_ref,
                          carry_ref, carryT_ref, *, q_scale, head_dim):
    @pl.when(pl.program_id(1) == 0)
    def _():
        carry_ref[...] = jnp.zeros_like(carry_ref)
        carryT_ref[...] = jnp.zeros_like(carryT_ref)

    tm = x_ref.shape[0]
    page = kT_ref.shape[-1]
    n_heads, depth, _ = qTa_ref.shape
    h = _modnorm(x_ref[...], g_ref[...], sh_ref[...], sc_ref[...]).astype(BF16)
    u_ref[...] = _dot(h, wu_ref[...])
    kT = _dot_nt(wkT_ref[...], h)
    vT = _dot_nt(wvT_ref[...], h)
    lf = _log_sigmoid(_dot(h, wf_ref[...]) + bf_ref[...])
    lfT = _log_sigmoid(_dot_nt(wfT_ref[...], h) + bfT_ref[...])
    for j in range(tm // page):
        cols = slice(j * page, (j + 1) * page)
        kT_ref[j] = kT[:, cols]
        vT_ref[j] = vT[:, cols]
        lfT_ref[j] = lfT[:, cols]
    r = lax.broadcasted_iota(jnp.int32, (tm, tm), 0)
    c = lax.broadcasted_iota(jnp.int32, (tm, tm), 1)
    cf = _dot_exact_right((c <= r).astype(BF16), lf) + carry_ref[...]
    carry_ref[...] = cf[tm - 1:tm, :]
    cfT = _dot_exact_left(lfT, (r <= c).astype(BF16)) + carryT_ref[:, :1]
    carryT_ref[...] = jnp.broadcast_to(cfT[:, tm - 1:tm], carryT_ref.shape)
    qT = _dot_nt(wqT_ref[...], h) * (q_scale * LOG2_E)
    k = kT.T
    sub = lax.broadcasted_iota(jnp.int32, (8, tm), 0)
    lane = lax.broadcasted_iota(jnp.int32, (tm, depth), 1)
    pad_rows = jnp.zeros((depth - head_dim - 8, tm), F32)
    sum_rows = vTa_ref.shape[1] - head_dim
    ones_row = jnp.where(lax.broadcasted_iota(jnp.int32, (sum_rows, tm), 0) == 0, 1.0, 0.0)
    for hd in range(n_heads):
        rows = slice(hd * head_dim, (hd + 1) * head_dim)
        vTa_ref[hd] = jnp.concatenate([vT[rows, :], ones_row], axis=0).astype(BF16)
        f_hi, f_mid, f_lo = _split3_f32(jnp.broadcast_to(cfT[hd:hd + 1, :] * LOG2_E, (8, tm)))
        q_bias = jnp.where(sub == 0, f_hi, jnp.where(sub == 1, f_mid, jnp.where(sub == 2, f_lo,
                 jnp.where(sub < 2 * N_BIAS_PARTS, 1.0, 0.0))))
        qTa_ref[hd] = jnp.concatenate([qT[rows, :], q_bias, pad_rows], axis=0).astype(BF16)
        n_hi, n_mid, n_lo = _split3_f32(jnp.broadcast_to(cf[:, hd:hd + 1] * -LOG2_E, (tm, depth)))
        k_bias = jnp.where(lane == head_dim + 3, n_hi, jnp.where(lane == head_dim + 4, n_mid,
                 jnp.where(lane == head_dim + 5, n_lo, jnp.where(lane < head_dim + N_BIAS_PARTS, 1.0, 0.0))))
        blk = k[:, (hd // 2) * depth:(hd // 2 + 1) * depth]
        if hd % 2:
            blk = pltpu.roll(blk, head_dim, 1)
        ka_ref[hd] = jnp.where(lane < head_dim, blk, k_bias).astype(BF16)


def _inproj_prompt(x, g, mods, k0, wqT, wkT, wvT, wf, wfT, bf, bfT, wu, page, n_heads, head_dim, q_scale):
    b, s, d = x.shape
    a = wkT.shape[0]
    pw = wu.shape[1]
    tm = ROW_TILE
    npg = tm // page
    depth = 2 * head_dim
    assert depth == LANES and depth - head_dim >= 8

    def mod_spec(k):
        return pl.BlockSpec((None, 1, d), lambda bi, ti: (bi, 0, k))

    out_shape = (
        jax.ShapeDtypeStruct((b, n_heads, depth, s), BF16),
        jax.ShapeDtypeStruct((b, n_heads, s, depth), BF16),
        jax.ShapeDtypeStruct((b, n_heads, head_dim + PV_SUM_ROWS, s), BF16),
        jax.ShapeDtypeStruct((b, s // page, a, page), F32),
        jax.ShapeDtypeStruct((b, s // page, a, page), F32),
        jax.ShapeDtypeStruct((b, s // page, n_heads, page), F32),
        jax.ShapeDtypeStruct((b, s, pw), F32),
    )
    out_specs = (
        pl.BlockSpec((None, n_heads, depth, tm), lambda bi, ti: (bi, 0, 0, ti)),
        pl.BlockSpec((None, n_heads, tm, depth), lambda bi, ti: (bi, 0, ti, 0)),
        pl.BlockSpec((None, n_heads, head_dim + PV_SUM_ROWS, tm), lambda bi, ti: (bi, 0, 0, ti)),
        pl.BlockSpec((None, npg, a, page), lambda bi, ti: (bi, ti, 0, 0)),
        pl.BlockSpec((None, npg, a, page), lambda bi, ti: (bi, ti, 0, 0)),
        pl.BlockSpec((None, npg, n_heads, page), lambda bi, ti: (bi, ti, 0, 0)),
        pl.BlockSpec((None, tm, pw), lambda bi, ti: (bi, ti, 0)),
    )
    weights = (wqT, wkT, wvT, wf, wfT, bf, bfT, wu)
    in_specs = [
        pl.BlockSpec((None, tm, d), lambda bi, ti: (bi, ti, 0)),
        _resident((1, d)),
        mod_spec(k0),
        mod_spec(k0 + 1),
    ] + [_resident(w.shape) for w in weights]
    return pl.pallas_call(
        functools.partial(_inproj_prompt_kernel, q_scale=q_scale, head_dim=head_dim),
        grid=(b, s // tm),
        in_specs=in_specs,
        out_specs=out_specs,
        out_shape=out_shape,
        scratch_shapes=[pltpu.VMEM((1, LANES), F32), pltpu.VMEM((n_heads, LANES), F32)],
        compiler_params=_params(),
        name="inproj_prompt",
    )(x, g, mods, mods, *weights)


def _attn_prompt_kernel(qTa_ref, ka_ref, vTa_ref, o_ref, m_ref, acc_ref, s_ref, *, head_dim):
    qi = pl.program_id(2)
    tq = qTa_ref.shape[-1]
    tk = tq
    m_ref[...] = jnp.full_like(m_ref, -jnp.inf)
    acc_ref[...] = jnp.zeros_like(acc_ref)

    def scores_into(slot, j):
        off = pl.multiple_of(j * tk, tk)
        for i in range(2):
            s_ref[slot, i] = _dot(ka_ref[i, pl.ds(off, tk), :], qTa_ref[i])

    def consume(slot, j, masked):
        off = pl.multiple_of(j * tk, tk)
        for i in range(2):
            s = s_ref[slot, i]
            if masked:
                key = lax.broadcasted_iota(jnp.int32, (tk, tq), 0)
                qry = lax.broadcasted_iota(jnp.int32, (tk, tq), 1)
                s = jnp.where(key <= qry, s, -jnp.inf)
            m_prev = m_ref[i]
            m_new = jnp.maximum(m_prev, jnp.max(s, axis=0, keepdims=True))
            alpha = jnp.exp2(m_prev - m_new)
            p = jnp.exp2(s - m_new)
            acc_ref[i] = alpha * acc_ref[i] + _dot(vTa_ref[i, :, pl.ds(off, tk)], p.astype(BF16))
            m_ref[i] = m_new

    def two_full_blocks(jj, carry):
        scores_into(1, 2 * jj + 1)
        consume(0, 2 * jj, False)
        scores_into(0, 2 * jj + 2)
        consume(1, 2 * jj + 1, False)
        return carry

    scores_into(0, 0)
    lax.fori_loop(0, qi // 2, two_full_blocks, 0)

    @pl.when(qi % 2 == 0)
    def _():
        consume(0, qi, True)

    @pl.when(qi % 2 == 1)
    def _():
        scores_into(1, qi)
        consume(0, qi - 1, False)
        consume(1, qi, True)

    oT = jnp.concatenate(
        [acc_ref[i, :head_dim, :] / acc_ref[i, head_dim:head_dim + 1, :] for i in range(2)], axis=0)
    o_ref[...] = oT.T.astype(BF16)


def _attn_prompt(qTa, ka, vTa, head_dim):
    b, n_heads, depth, s = qTa.shape
    v_rows = vTa.shape[2]
    a = n_heads * head_dim
    tq = ATTN_TILE
    return pl.pallas_call(
        functools.partial(_attn_prompt_kernel, head_dim=head_dim),
        grid=(b, n_heads // 2, s // tq),
        in_specs=[
            pl.BlockSpec((None, 2, depth, tq), lambda bi, hp, qi: (bi, hp, 0, qi)),
            pl.BlockSpec((None, 2, s, depth), lambda bi, hp, qi: (bi, hp, 0, 0)),
            pl.BlockSpec((None, 2, v_rows, s), lambda bi, hp, qi: (bi, hp, 0, 0)),
        ],
        out_specs=pl.BlockSpec((None, tq, depth), lambda bi, hp, qi: (bi, qi, hp)),
        out_shape=jax.ShapeDtypeStruct((b, s, a), BF16),
        scratch_shapes=[
            pltpu.VMEM((2, 1, tq), F32),
            pltpu.VMEM((2, v_rows, tq), F32),
            pltpu.VMEM((2, 2, tq, tq), F32),
        ],
        compiler_params=_params(),
        name="attn_prompt",
    )(qTa, ka, vTa)


def _pool_prompt_kernel(u_ref, wp_ref, ps_ref, o_ref, halo_ref):
    ti = pl.program_id(1)

    @pl.when(ti == 0)
    def _():
        halo_ref[...] = jnp.zeros_like(halo_ref)

    tm = u_ref.shape[0]
    nh = halo_ref.shape[0]
    gd = wp_ref.shape[1]
    u = u_ref[...]
    ext = jnp.concatenate([halo_ref[...], u], axis=0)
    halo_ref[...] = u[tm - nh:, :]
    sums = {1: ext}
    w = 1
    while w < max(POOL_WINDOWS):
        sums[2 * w] = sums[w] + pltpu.roll(sums[w], w, 0)
        w *= 2
    pos = ti * tm + lax.broadcasted_iota(jnp.int32, (tm, 1), 0)
    ys = []
    for gi, w in enumerate(POOL_WINDOWS):
        lanes = slice(gi * gd, (gi + 1) * gd)
        count = jnp.minimum(pos + 1, w).astype(F32)
        d = sums[w][nh:, lanes] / count - u[:, lanes]
        ys.append(_dot(d.astype(BF16), wp_ref[gi]))
    o_ref[...] = (jnp.concatenate(ys, axis=1) * ps_ref[...]).astype(BF16)


def _pool_prompt(u, wp, ps):
    b, s, pw = u.shape
    tm = ROW_TILE
    halo = 16
    assert halo >= max(POOL_WINDOWS)
    return pl.pallas_call(
        _pool_prompt_kernel,
        grid=(b, s // tm),
        in_specs=[
            pl.BlockSpec((None, tm, pw), lambda bi, ti: (bi, ti, 0)),
            _resident(wp.shape),
            _resident(ps.shape),
        ],
        out_specs=pl.BlockSpec((None, tm, pw), lambda bi, ti: (bi, ti, 0)),
        out_shape=jax.ShapeDtypeStruct((b, s, pw), BF16),
        scratch_shapes=[pltpu.VMEM((halo, pw), F32)],
        compiler_params=_params(),
        name="pool_prompt",
    )(u, wp, ps)


def _outproj_kernel(x_ref, a_ref, p_ref, gt_ref, wa_ref, wp_ref, o_ref):
    x = x_ref[...]
    mix = _dot(a_ref[...], wa_ref[...]) + _dot(p_ref[...], wp_ref[...])
    o_ref[...] = x + _rows(gt_ref[...], x.shape[0]) * mix


def _outproj(x, attn, pool, mods, mod_spec, k, wa, wp):
    r, d = x.shape
    tm = ROW_TILE
    return pl.pallas_call(
        _outproj_kernel,
        grid=(r // tm,),
        in_specs=[
            pl.BlockSpec((tm, d), lambda i: (i, 0)),
            pl.BlockSpec((tm, attn.shape[1]), lambda i: (i, 0)),
            pl.BlockSpec((tm, pool.shape[1]), lambda i: (i, 0)),
            mod_spec(k),
            _resident(wa.shape),
            _resident(wp.shape),
        ],
        out_specs=pl.BlockSpec((tm, d), lambda i: (i, 0)),
        out_shape=jax.ShapeDtypeStruct((r, d), F32),
        compiler_params=_params(),
        name="outproj",
    )(x, attn, pool, mods, wa, wp)


def _inproj_sample_kernel(x_ref, g_ref, sh_ref, sc_ref, wqkv_ref, wkT_ref, wvT_ref, wf_ref, wfT_ref,
                          bf_ref, bfT_ref, wu_ref,
                          q_ref, k_ref, v_ref, kT_ref, vT_ref, lfT_ref, c_ref, u_ref, *, q_scale):
    n_t, _, nb = kT_ref.shape
    n_heads = c_ref.shape[-1]
    h = _modnorm(x_ref[...], g_ref[...], sh_ref[...], sc_ref[...]).astype(BF16)
    z = _dot(h, wqkv_ref[...])
    a = z.shape[1] // 3
    q_ref[...] = (z[:, :a] * q_scale).astype(BF16)
    k_ref[...] = z[:, a:2 * a]
    v_ref[...] = z[:, 2 * a:]
    u_ref[...] = _dot(h, wu_ref[...])
    kT = _dot_nt(wkT_ref[...], h)
    vT = _dot_nt(wvT_ref[...], h)
    lf = _log_sigmoid(_dot(h, wf_ref[...]) + bf_ref[...])
    lfT = _log_sigmoid(_dot_nt(wfT_ref[...], h) + bfT_ref[...])
    run = None
    parts = []
    for t in range(n_t):
        cols = slice(t * nb, (t + 1) * nb)
        kT_ref[t] = kT[:, cols]
        vT_ref[t] = vT[:, cols]
        lfT_ref[t] = lfT[:, cols]
        blk = lf[t * nb:(t + 1) * nb, :]
        run = blk if run is None else run + blk
        parts.append(run)
    c_ref[...] = jnp.concatenate(parts, axis=0)[:, :n_heads]


def _inproj_sample(x, g, mods, mod_spec, k0, wqkv, wkT, wvT, wf, wfT, bf, bfT, wu, n_t, n_heads, q_scale):
    r, d = x.shape
    a = wkT.shape[0]
    pw = wu.shape[1]
    nb = r // n_t
    out_shape = (
        jax.ShapeDtypeStruct((r, a), BF16),
        jax.ShapeDtypeStruct((r, a), F32),
        jax.ShapeDtypeStruct((r, a), F32),
        jax.ShapeDtypeStruct((n_t, a, nb), F32),
        jax.ShapeDtypeStruct((n_t, a, nb), F32),
        jax.ShapeDtypeStruct((n_t, n_heads, nb), F32),
        jax.ShapeDtypeStruct((r, n_heads), F32),
        jax.ShapeDtypeStruct((r, pw), F32),
    )
    whole = lambda shp: pl.BlockSpec(shp, lambda i: (0,) * len(shp))
    in_specs = [whole((r, d)), whole((1, d)), mod_spec(k0), mod_spec(k0 + 1)] + [
        whole(w.shape) for w in (wqkv, wkT, wvT, wf, wfT, bf, bfT, wu)
    ]
    return pl.pallas_call(
        functools.partial(_inproj_sample_kernel, q_scale=q_scale),
        grid=(1,),
        in_specs=in_specs,
        out_specs=tuple(whole(o.shape) for o in out_shape),
        out_shape=out_shape,
        compiler_params=_params(),
        name="inproj_sample",
    )(x, g, mods, mods, wqkv, wkT, wvT, wf, wfT, bf, bfT, wu)


def _decode_kernel(pt_ref, q_ref, cq_ref, c2_ref, kn_ref, vn_ref, kc_ref, vc_ref, lfc_ref,
                   x_ref, g_ref, sh_ref, sc_ref, gt_ref, w1_ref, w3_ref, w2_ref, *rest,
                   head_dim, ffn_tiles, final_norm):
    gf_ref = rest[0] if final_norm else None
    o_ref, y_ref, kbuf, vbuf, lfbuf, sems, m_ref, l_ref, acc_ref, r_ref, h_ref, facc_ref = rest[int(final_norm):]
    g_n = kbuf.shape[1]
    seq = pl.program_id(0)
    step = pl.program_id(1)
    n_steps = pl.num_programs(1)
    n_pages = n_steps * g_n
    lin = seq * n_steps + step
    slot = lin % 2

    def page_copies(seq_i, step_i, slot_i):
        copies = []
        for i in range(g_n):
            pid = pt_ref[seq_i, n_pages - 1 - (step_i * g_n + i)]
            copies.append(pltpu.make_async_copy(kc_ref.at[pid], kbuf.at[slot_i, i], sems.at[slot_i, 0, i]))
            copies.append(pltpu.make_async_copy(vc_ref.at[pid], vbuf.at[slot_i, i], sems.at[slot_i, 1, i]))
            copies.append(pltpu.make_async_copy(lfc_ref.at[pid], lfbuf.at[slot_i, i], sems.at[slot_i, 2, i]))
        return copies

    @pl.when(lin == 0)
    def _():
        for n, cp in enumerate(page_copies(0, 0, 0)):
            cp.start(priority=COPY_PRIORITY[n % len(COPY_PRIORITY)])

    @pl.when(lin + 1 < pl.num_programs(0) * n_steps)
    def _():
        wrap = step + 1 == n_steps
        for n, cp in enumerate(page_copies(jnp.where(wrap, seq + 1, seq), jnp.where(wrap, 0, step + 1), 1 - slot)):
            cp.start(priority=COPY_PRIORITY[n % len(COPY_PRIORITY)])

    n_chunks = w1_ref.shape[1] // FFN_CHUNK
    chunk = lin % n_chunks

    @pl.when(lin < ffn_tiles * n_chunks)
    def _():
        @pl.when(chunk == 0)
        def _():
            _ffn_begin(x_ref, g_ref, sh_ref, sc_ref, h_ref, facc_ref)

        _ffn_chunk(chunk, h_ref, facc_ref, w1_ref, w3_ref, w2_ref)

        @pl.when(chunk == n_chunks - 1)
        def _():
            y = _ffn_end(x_ref, gt_ref, facc_ref)
            y_ref[...] = _rmsnorm(y, gf_ref[...]) if final_norm else y

    for cp in page_copies(seq, step, slot):
        cp.wait()
    k_refs = [kbuf.at[slot, i] for i in range(g_n)]
    v_refs = [vbuf.at[slot, i] for i in range(g_n)]
    lf_refs = [lfbuf.at[slot, i] for i in range(g_n)]
    n_rows, width = q_ref.shape
    n_heads = width // head_dim
    n_t = n_rows // n_heads
    row = lax.broadcasted_iota(jnp.int32, (n_rows, 1), 0)
    lane = lax.broadcasted_iota(jnp.int32, (1, width), 1)
    own = (lane // head_dim) == (row % n_heads)
    t_row = row // n_heads
    q = q_ref[...]
    qbd = jnp.where(own, q, jnp.zeros_like(q))
    cq = cq_ref[...]

    @pl.when(step == 0)
    def _():
        qf = qbd.astype(F32)
        kn = kn_ref[...]
        vn = vn_ref[...]
        c2 = c2_ref[...]
        sc = []
        for s in range(n_t):
            col = jnp.sum(qf * kn[s:s + 1, :], axis=1, keepdims=True) + cq - c2[:, s:s + 1]
            sc.append(jnp.where(t_row >= s, col, -jnp.inf))
        m0 = functools.reduce(jnp.maximum, sc)
        l0 = jnp.zeros_like(m0)
        a0 = jnp.zeros(acc_ref.shape, F32)
        for s in range(n_t):
            p = jnp.exp(sc[s] - m0)
            l0 = l0 + p
            a0 = a0 + p * vn[s:s + 1, :]
        m_ref[...] = m0
        l_ref[...] = l0
        acc_ref[...] = a0
        r_ref[...] = jnp.zeros_like(r_ref)

    page = lf_refs[0].shape[-1]
    pj = lax.broadcasted_iota(jnp.int32, (page, page), 0)
    ps = lax.broadcasted_iota(jnp.int32, (page, page), 1)
    later = (pj > ps).astype(BF16)
    r_run = r_ref[...]
    lf_all = jnp.concatenate([lf_refs[i][...] for i in range(g_n)], axis=0)
    suffix_all = _dot_exact_left(lf_all, later)
    totals = jnp.sum(lf_all, axis=1, keepdims=True)
    scores = []
    for i in range(g_n):
        rows = slice(i * n_heads, (i + 1) * n_heads)
        suffix = suffix_all[rows] + r_run
        r_run = r_run + totals[rows]
        bias = jnp.concatenate([suffix] * n_t, axis=0) + cq
        scores.append(_dot(qbd, k_refs[i][...].astype(BF16)) + bias)
    r_ref[...] = r_run
    m_prev = m_ref[...]
    m_new = functools.reduce(jnp.maximum, [m_prev] + [jnp.max(s, axis=1, keepdims=True) for s in scores])
    alpha = jnp.exp(m_prev - m_new)
    l_new = alpha * l_ref[...]
    acc = alpha * acc_ref[...]
    for i in range(g_n):
        p = jnp.exp(scores[i] - m_new)
        l_new = l_new + jnp.sum(p, axis=1, keepdims=True)
        acc = acc + _dot_nt(p.astype(BF16), v_refs[i][...].astype(BF16))
    m_ref[...] = m_new
    l_ref[...] = l_new
    acc_ref[...] = acc

    @pl.when(step == pl.num_programs(1) - 1)
    def _():
        o = jnp.where(own, acc, 0.0) / l_new
        o_ref[...] = o.reshape(n_t, n_heads, width).sum(axis=1)


def _decode_and_ffn(page_table, q_rep, cq, c2, k_new, v_new, cache_kT, cache_vT, cache_lfT, head_dim,
                    x, g, mods, k0, tiles_per_seq, w1, w3, w2, gf=None, ffn_tiles=None, in_place=False):
    bd, n_pages = page_table.shape
    _, n_rows, width = q_rep.shape
    n_t = k_new.shape[1]
    page = cache_kT.shape[-1]
    n_heads = cache_lfT.shape[1]
    g_n = DECODE_PAGES_PER_STEP
    n_steps = n_pages // g_n
    r, d = x.shape
    dff = w1.shape[1]
    tm = ROW_TILE
    ffn_tiles = r // tm if ffn_tiles is None else ffn_tiles
    n_chunks = dff // FFN_CHUNK
    assert ffn_tiles * n_chunks <= bd * n_steps, "not enough decode steps to carry the SwiGLU chunks"

    def tile(b, s):
        return jnp.minimum((b * n_steps + s) // n_chunks, ffn_tiles - 1)

    def mod_spec(k):
        return pl.BlockSpec((None, 1, d), lambda b, s, pt: (tile(b, s) // tiles_per_seq, 0, k))

    per_seq = lambda shp: pl.BlockSpec((None,) + shp, lambda b, s, pt: (b, 0, 0))
    in_hbm = pl.BlockSpec(memory_space=pl.ANY)
    row_tile = pl.BlockSpec((tm, d), lambda b, s, pt: (tile(b, s), 0))
    in_specs = [
        per_seq((n_rows, width)),
        per_seq((n_rows, 1)),
        per_seq((n_rows, n_t)),
        per_seq((n_t, width)),
        per_seq((n_t, width)),
        in_hbm,
        in_hbm,
        in_hbm,
        row_tile,
        _resident((1, d)),
        mod_spec(k0),
        mod_spec(k0 + 1),
        mod_spec(k0 + 2),
        _resident((d, dff)),
        _resident((d, dff)),
        _resident((dff, d)),
    ]
    args = [page_table, q_rep, cq, c2, k_new, v_new, cache_kT, cache_vT, cache_lfT]
    x_operand = len(args)
    args += [x, g, mods, mods, mods, w1, w3, w2]
    if gf is not None:
        in_specs.append(_resident((1, d)))
        args.append(gf)
    grid_spec = pltpu.PrefetchScalarGridSpec(
        num_scalar_prefetch=1,
        grid=(bd, n_steps),
        in_specs=in_specs,
        out_specs=(per_seq((n_t, width)), row_tile),
        scratch_shapes=[
            pltpu.VMEM((2, g_n, width, page), F32),
            pltpu.VMEM((2, g_n, width, page), F32),
            pltpu.VMEM((2, g_n, n_heads, page), F32),
            pltpu.SemaphoreType.DMA((2, 3, g_n)),
            pltpu.VMEM((n_rows, 1), F32),
            pltpu.VMEM((n_rows, 1), F32),
            pltpu.VMEM((n_rows, width), F32),
            pltpu.VMEM((n_heads, LANES), F32),
            pltpu.VMEM((tm, d), BF16),
            pltpu.VMEM((tm, d), F32),
        ],
    )
    return pl.pallas_call(
        functools.partial(_decode_kernel, head_dim=head_dim, ffn_tiles=ffn_tiles, final_norm=gf is not None),
        grid_spec=grid_spec,
        out_shape=(jax.ShapeDtypeStruct((bd, n_t, width), F32), jax.ShapeDtypeStruct((r, d), F32)),
        input_output_aliases={x_operand: 1} if in_place else {},
        compiler_params=_params(),
        name="decode_attn_ffn",
    )(*args)


def _pool_sample_kernel(st_ref, u_ref, wp_ref, ps_ref, o_ref):
    n_state = st_ref.shape[0]
    n_t = u_ref.shape[0]
    gd = wp_ref.shape[1]

    def rows(idx, lanes):
        return st_ref[idx, :, lanes] if idx < n_state else u_ref[idx - n_state, :, lanes]

    for t in range(n_t):
        ys = []
        for gi, w in enumerate(POOL_WINDOWS):
            lanes = slice(gi * gd, (gi + 1) * gd)
            cur = rows(n_state + t, lanes)
            total = cur
            for j in range(1, w):
                total = total + rows(n_state + t - j, lanes)
            d = total / float(w) - cur
            ys.append(_dot(d.astype(BF16), wp_ref[gi]))
        o_ref[t] = (jnp.concatenate(ys, axis=1) * ps_ref[...]).astype(BF16)


def _pool_sample(state_tm, u_tm, wp, ps):
    assert state_tm.shape[0] >= max(POOL_WINDOWS) - 1
    whole = lambda shp: pl.BlockSpec(shp, lambda i: (0,) * len(shp))
    return pl.pallas_call(
        _pool_sample_kernel,
        grid=(1,),
        in_specs=[whole(state_tm.shape), whole(u_tm.shape), whole(wp.shape), whole(ps.shape)],
        out_specs=whole(u_tm.shape),
        out_shape=jax.ShapeDtypeStruct(u_tm.shape, BF16),
        compiler_params=_params(),
        name="pool_sample",
    )(state_tm, u_tm, wp, ps)


def kernel(x_prompt, x_sample, cache_k, cache_v, cache_logf, state_pool, page_table, c_prompt, c_sample, norm_ffn1, norm_mix, norm_ffn2, norm_final, w_ada, b_ada, ffn1_w1, ffn1_w3, ffn1_w2, ffn2_w1, ffn2_w3, ffn2_w2, w_in, b_f, w_o, w_pool, pool_scale):
    bp, s_len, d = x_prompt.shape
    bd, t_len, _ = x_sample.shape
    depth, n_phys, page, n_heads, head_dim = cache_k.shape
    assert depth == 1, "single-layer model"
    a = n_heads * head_dim
    pw = state_pool.shape[-1]
    n_state = state_pool.shape[2]
    q_scale = head_dim ** -0.5
    li = 0

    n_pad = (-(bd + bp)) % 16
    c_all = jnp.concatenate([c_sample, c_prompt, jnp.zeros((n_pad, d), F32)], axis=0)
    mods = _ada(c_all, w_ada[li], b_ada[li][None, :])
    mods_p = mods[bd:bd + bp].reshape(bp, 1, N_MODS * d)
    tiles_per_seq = s_len // ROW_TILE

    def prompt_mod_from(first_tile):
        return lambda k: pl.BlockSpec((None, 1, d), lambda i: ((i + first_tile) // tiles_per_seq, 0, k))

    prompt_mod = prompt_mod_from(0)

    def sample_mod(k):
        return pl.BlockSpec((bd, d), lambda i: (0, k))

    f1 = [w[li].astype(BF16) for w in (ffn1_w1, ffn1_w3, ffn1_w2)]
    f2 = [w[li].astype(BF16) for w in (ffn2_w1, ffn2_w3, ffn2_w2)]
    w_in_l = w_in[li]
    wqkv = w_in_l[:, :3 * a].astype(BF16)
    wqT = w_in_l[:, :a].T.astype(BF16)
    wkT = w_in_l[:, a:2 * a].T.astype(BF16)
    wvT = w_in_l[:, 2 * a:3 * a].T.astype(BF16)
    wf_cols = w_in_l[:, 3 * a:3 * a + n_heads]
    wf = jnp.pad(wf_cols, ((0, 0), (0, LANES - n_heads))).astype(BF16)
    wfT = wf_cols.T.astype(BF16)
    bf = jnp.pad(b_f[li][None, :], ((0, 0), (0, LANES - n_heads)))
    bfT = b_f[li][:, None]
    wu = w_in_l[:, 3 * a + n_heads:].astype(BF16)
    wo_a = w_o[li][:a].astype(BF16)
    wo_p = w_o[li][a:].astype(BF16)
    wp = w_pool[li].astype(BF16)
    ps = pool_scale[li][None, :]
    g1 = norm_ffn1[li][None, :]
    gm = norm_mix[li][None, :]
    g2 = norm_ffn2[li][None, :]
    gf = norm_final[None, :]

    n_rows = bd * t_len
    xs = x_sample.transpose(1, 0, 2).reshape(n_rows, d)
    xs = _ffn(xs, g1, mods, sample_mod, 0, *f1)
    q_s, k_s, v_s, kT_s, vT_s, lfT_s, c_s, u_s = _inproj_sample(
        xs, gm, mods, sample_mod, 3, wqkv, wkT, wvT, wf, wfT, bf, bfT, wu, t_len, n_heads, q_scale)

    def per_seq(z):
        return z.reshape(t_len, bd, z.shape[-1]).transpose(1, 0, 2)

    q_rep = jnp.repeat(per_seq(q_s), n_heads, axis=1)
    c_bt = per_seq(c_s)
    cq = c_bt.reshape(bd, t_len * n_heads, 1)
    c2 = jnp.broadcast_to(c_bt.transpose(0, 2, 1)[:, None], (bd, t_len, n_heads, t_len)).reshape(bd, t_len * n_heads, t_len)
    cache_kT = cache_k[li].transpose(0, 2, 3, 1).reshape(n_phys, a, page)
    cache_vT = cache_v[li].transpose(0, 2, 3, 1).reshape(n_phys, a, page)
    cache_lfT = cache_logf[li].transpose(0, 2, 1)

    k_new, v_new = per_seq(k_s), per_seq(v_s)
    n_tiles = bp * s_len // ROW_TILE
    n_steps = page_table.shape[1] // DECODE_PAGES_PER_STEP
    n_chunks = f1[0].shape[1] // FFN_CHUNK
    seq_a = min(bd, -(-n_tiles * n_chunks // n_steps))
    tiles_b = min(n_tiles, (bd - seq_a) * n_steps // n_chunks)
    if tiles_b == 0:
        seq_a = bd

    def decode_part(lo, hi, x, g, k0, w, **kw):
        sl = slice(lo, hi)
        return _decode_and_ffn(page_table[sl], q_rep[sl], cq[sl], c2[sl], k_new[sl], v_new[sl],
                               cache_kT, cache_vT, cache_lfT, head_dim, x, g, mods_p, k0, tiles_per_seq, *w, **kw)

    attn_s, xp = decode_part(0, seq_a, x_prompt.reshape(bp * s_len, d), g1, 0, f1)

    qTa_p, ka_p, vTa_p, kT_p, vT_p, lfT_p, u_p = _inproj_prompt(
        xp.reshape(bp, s_len, d), gm, mods_p, 3, wqT, wkT, wvT, wf, wfT, bf, bfT, wu,
        page, n_heads, head_dim, q_scale)
    attn_p = _attn_prompt(qTa_p, ka_p, vTa_p, head_dim)
    pool_p = _pool_prompt(u_p, wp, ps)
    xp = _outproj(xp, attn_p.reshape(bp * s_len, a), pool_p.reshape(bp * s_len, pw), mods_p, prompt_mod, 5, wo_a, wo_p)
    if tiles_b:
        attn_b, y_part = decode_part(seq_a, bd, xp, g2, 6, f2, gf=gf, ffn_tiles=tiles_b, in_place=True)
        attn_s = jnp.concatenate([attn_s, attn_b], axis=0)
        y_prompt = _ffn(y_part, g2, mods_p, prompt_mod_from(tiles_b), 6, *f2, gf=gf, first_tile=tiles_b, in_place=True)
    else:
        y_prompt = _ffn(xp, g2, mods_p, prompt_mod, 6, *f2, gf=gf)
    y_prompt = y_prompt.reshape(bp, s_len, d)

    n_pp = s_len // page
    new_k_prompt = kT_p.reshape(bp, n_pp, n_heads, head_dim, page).transpose(0, 1, 4, 2, 3)[None]
    new_v_prompt = vT_p.reshape(bp, n_pp, n_heads, head_dim, page).transpose(0, 1, 4, 2, 3)[None]
    new_logf_prompt = lfT_p.transpose(0, 1, 3, 2)[None]
    new_pool_prompt = u_p[:, s_len - n_state:][None]

    attn_s = attn_s.transpose(1, 0, 2).reshape(n_rows, a).astype(BF16)
    state_tm = state_pool[li].transpose(1, 0, 2)
    u_tm = u_s.reshape(t_len, bd, pw)
    pool_s = _pool_sample(state_tm, u_tm, wp, ps).reshape(n_rows, pw)
    xs = _outproj(xs, attn_s, pool_s, mods, sample_mod, 5, wo_a, wo_p)
    y_sample = _ffn(xs, g2, mods, sample_mod, 6, *f2, gf=gf).reshape(t_len, bd, d).transpose(1, 0, 2)

    new_k_sample = kT_s.reshape(t_len, n_heads, head_dim, bd).transpose(3, 0, 1, 2)[None]
    new_v_sample = vT_s.reshape(t_len, n_heads, head_dim, bd).transpose(3, 0, 1, 2)[None]
    new_logf_sample = lfT_s.transpose(2, 0, 1)[None]
    new_pool_sample = jnp.concatenate([state_tm[t_len:], u_tm], axis=0).transpose(1, 0, 2)[None]

    return (y_prompt, y_sample, new_k_prompt, new_v_prompt, new_logf_prompt, new_pool_prompt,
            new_k_sample, new_v_sample, new_logf_sample, new_pool_sample)
```
